```python
import jax, jax.numpy as jnp
from jax import lax
import numpy as np

D_MODEL = 1024
BATCH = 32
SEQ = 2048
DEPTH = 1
DEC_BATCH = 32
DEC_SEQ = 64
PAST_LEN = 4096

CHUNK = 64
N_META = 16
Q_BLOCK = 128
M_HEADS = 4
M_DK = D_MODEL // M_HEADS
M_DV = D_MODEL // M_HEADS
M_QK = M_HEADS * M_DK
M_V = M_HEADS * M_DV
CONV_W = 4
A_HEADS = 8
NOPE_DIM = 128
ROPE_DIM = 64
V_HEAD = 128
Q_LORA = 3 * D_MODEL // 8
KV_LORA = D_MODEL // 4
ROPE_THETA = 10000.0
ATTN_SCALE = (NOPE_DIM + ROPE_DIM) ** -0.5
N_EXPERTS = 32
TOP_K = 4
D_FF = D_MODEL
SWIGLU_LIMIT = 7.0
SWIGLU_ALPHA = 1.702
MOE_BLOCK = 128
EPS = 1e-6
IN_SIZES = (2 * M_QK, M_V, M_V, M_HEADS, M_HEADS, Q_LORA, KV_LORA, ROPE_DIM, D_MODEL, D_MODEL)
IN_SPLITS = tuple(int(s) for s in np.cumsum(IN_SIZES)[:-1])
D_IN = int(sum(IN_SIZES))

kernel_name = 'hybrid_mlstm_mla_moe_stream_step'


def rmsnorm(x, g):
    xf = x.astype(jnp.float32)
    y = xf * lax.rsqrt(jnp.mean(xf * xf, axis=-1, keepdims=True) + EPS)
    return (y * g.astype(jnp.float32)).astype(x.dtype)


def rope(x, pos):
    half = x.shape[-1] // 2
    freqs = ROPE_THETA ** (-jnp.arange(half, dtype=jnp.float32) / half)
    ang = pos.astype(jnp.float32)[:, None] * freqs[None, :]
    shape = (ang.shape[0],) + (1,) * (x.ndim - 3) + (half,)
    cos = jnp.cos(ang).reshape(shape)
    sin = jnp.sin(ang).reshape(shape)
    xf = x.astype(jnp.float32)
    x1, x2 = xf[..., :half], xf[..., half:]
    return jnp.concatenate([x1 * cos - x2 * sin, x1 * sin + x2 * cos], axis=-1).astype(x.dtype)


def causal_conv(u, buf, w, b):
    xp = jnp.concatenate([buf.astype(u.dtype), u], axis=1)
    t = u.shape[1]
    out = b
    for j in range(CONV_W):
        out = out + xp[:, j:j + t] * w[j]
    return out, xp[:, xp.shape[1] - (CONV_W - 1):]


def mlstm_chunk(C, n, m, q, k, v, ig, lf):
    L = q.shape[1]
    b = jnp.swapaxes(jnp.cumsum(lf, axis=1), 1, 2)
    it = jnp.swapaxes(ig, 1, 2)
    causal = jnp.tril(jnp.ones((L, L), dtype=bool))
    d = jnp.where(causal, b[..., :, None] - b[..., None, :] + it[..., None, :], -jnp.inf)
    inter = b + m[..., None]
    m_t = jnp.maximum(inter, jnp.max(d, axis=-1))
    w_intra = jnp.exp(d - m_t[..., None])
    w_inter = jnp.exp(inter - m_t)
    s = jnp.einsum('blhd,bshd->bhls', q, k) * w_intra
    num = (jnp.einsum('bhls,bshe->blhe', s, v)
           + jnp.swapaxes(w_inter, 1, 2)[..., None] * jnp.einsum('blhd,bhde->blhe', q, C))
    qn = jnp.sum(s, axis=-1) + w_inter * jnp.einsum('blhd,bhd->bhl', q, n)
    den = jnp.maximum(jnp.abs(qn), jnp.exp(-m_t))
    h = num / jnp.swapaxes(den, 1, 2)[..., None]
    m_new = m_t[..., -1]
    w_prev = jnp.exp(b[..., -1] + m - m_new)
    w_k = jnp.exp(b[..., -1:] - b + it - m_new[..., None])
    C_new = w_prev[..., None, None] * C + jnp.einsum('bhs,bshd,bshe->bhde', w_k, k, v)
    n_new = w_prev[..., None] * n + jnp.einsum('bhs,bshd->bhd', w_k, k)
    return h, C_new, n_new, m_new


def mlstm_scan(C, n, m, q, k, v, ig, lf):
    B, T = q.shape[0], q.shape[1]
    nc = T // CHUNK

    def to_chunks(a):
        return jnp.moveaxis(a.reshape((B, nc, CHUNK) + a.shape[2:]), 1, 0)

    def step(carry, xs):
        h, C1, n1, m1 = mlstm_chunk(carry[0], carry[1], carry[2], *xs)
        return (C1, n1, m1), h

    (C, n, m), hs = lax.scan(step, (C, n, m), tuple(to_chunks(a) for a in (q, k, v, ig, lf)))
    h = jnp.moveaxis(hs, 0, 1).reshape((B, T) + hs.shape[3:])
    return h, C, n, m


def mla_attend(q_lat, q_rope, lat, kr, mask):
    s = (jnp.einsum('bqhr,bkr->bhqk', q_lat, lat)
         + jnp.einsum('bqhp,bkp->bhqk', q_rope, kr)).astype(jnp.float32) * ATTN_SCALE
    s = jnp.where(mask, s, -jnp.inf)
    p = jax.nn.softmax(s, axis=-1).astype(lat.dtype)
    return jnp.einsum('bhqk,bkr->bqhr', p, lat)


def moe(x2, w_router, b_router, w_gu, b_gu, w_dn, b_dn):
    M = x2.shape[0]
    logits = (x2 @ w_router).astype(jnp.float32) + b_router.astype(jnp.float32)
    top_v, top_i = lax.top_k(logits, TOP_K)
    gates = jax.nn.softmax(top_v, axis=-1)
    A = M * TOP_K
    flat_e = top_i.reshape(A)
    order = jnp.argsort(flat_e)
    se = flat_e[order]
    stok = order // TOP_K
    sgate = gates.reshape(A)[order]
    counts = jnp.zeros((N_EXPERTS,), jnp.int32).at[flat_e].add(1)
    pcounts = (counts + MOE_BLOCK - 1) // MOE_BLOCK * MOE_BLOCK
    start = jnp.cumsum(counts) - counts
    pend = jnp.cumsum(pcounts)
    pstart = pend - pcounts
    dest = pstart[se] + jnp.arange(A, dtype=jnp.int32) - start[se]
    n_blk = -(-A // MOE_BLOCK) + N_EXPERTS
    xs = jnp.zeros((n_blk * MOE_BLOCK, x2.shape[1]), x2.dtype).at[dest].set(x2[stok])
    blk_e = jnp.minimum(jnp.searchsorted(pend, jnp.arange(n_blk, dtype=jnp.int32) * MOE_BLOCK, side='right'),
                        N_EXPERTS - 1)

    def expert_block(args):
        xb, e = args
        gu = xb @ w_gu[e] + b_gu[e]
        gate = jnp.minimum(gu[:, :D_FF], SWIGLU_LIMIT)
        up = jnp.clip(gu[:, D_FF:], -SWIGLU_LIMIT, SWIGLU_LIMIT)
        hid = (up + 1.0) * gate * jax.nn.sigmoid(SWIGLU_ALPHA * gate)
        return hid @ w_dn[e] + b_dn[e]

    ys = lax.map(expert_block, (xs.reshape(n_blk, MOE_BLOCK, x2.shape[1]), blk_e))
    ys = ys.reshape(n_blk * MOE_BLOCK, x2.shape[1])
    return jnp.zeros_like(x2).at[stok].add(ys[dest] * sgate[:, None].astype(x2.dtype))


def trunk_layer(x, rope_pos, conv_buf, C0, n0, m0, lat_past, kr_past, prompt, lw):
    (g_mix, w_in, b_if, w_conv, b_conv, g_mh, w_proj_a, g_q, g_kv, w_uq, w_uk, w_uv, w_proj_b,
     w_out, g_ffn, w_router, b_router, w_gu, b_gu, w_dn, b_dn) = lw
    B, T, _ = x.shape
    z = rmsnorm(x, g_mix) @ w_in
    qk_pre, v_m, o_m, i_pre, f_pre, c_q, c_kv, k_r, g_a, g_b = jnp.split(z, IN_SPLITS, axis=-1)

    qk, new_buf = causal_conv(qk_pre, conv_buf, w_conv, b_conv)
    qk = jax.nn.silu(qk).astype(jnp.float32)
    q_m = qk[..., :M_QK].reshape(B, T, M_HEADS, M_DK)
    k_m = qk[..., M_QK:].reshape(B, T, M_HEADS, M_DK) * (M_DK ** -0.5)
    v_h = v_m.astype(jnp.float32).reshape(B, T, M_HEADS, M_DV)
    ig = i_pre.astype(jnp.float32) + b_if[:M_HEADS].astype(jnp.float32)
    lf = jax.nn.log_sigmoid(f_pre.astype(jnp.float32) + b_if[M_HEADS:].astype(jnp.float32))
    C0 = C0.astype(jnp.float32)
    n0 = n0.astype(jnp.float32)
    m0 = m0.astype(jnp.float32)
    if prompt:
        h0, C, n, m = mlstm_chunk(C0, n0, m0, q_m[:, :N_META], k_m[:, :N_META], v_h[:, :N_META],
                                  ig[:, :N_META], lf[:, :N_META])
        h1, C, n, m = mlstm_scan(C, n, m, q_m[:, N_META:], k_m[:, N_META:], v_h[:, N_META:],
                                 ig[:, N_META:], lf[:, N_META:])
        hm = jnp.concatenate([h0, h1], axis=1)
    else:
        hm, C, n, m = mlstm_chunk(C0, n0, m0, q_m, k_m, v_h, ig, lf)
    hm = hm * lax.rsqrt(jnp.mean(hm * hm, axis=-1, keepdims=True) + EPS)
    hm = (hm * g_mh.astype(jnp.float32).reshape(M_HEADS, M_DV)).reshape(B, T, M_V).astype(x.dtype)
    y_a = (hm * jax.nn.sigmoid(o_m)) @ w_proj_a

    q = (rmsnorm(c_q, g_q) @ w_uq).reshape(B, T, A_HEADS, NOPE_DIM + ROPE_DIM)
    q_nope = q[..., :NOPE_DIM]
    q_rope = rope(q[..., NOPE_DIM:], rope_pos)
    q_lat = jnp.einsum('bthn,rhn->bthr', q_nope, w_uk)
    lat = rmsnorm(c_kv, g_kv)
    kr = rope(k_r, rope_pos)
    if prompt:
        p = jnp.arange(T)
        kc = jnp.where(p < N_META, -1, (p - N_META) // CHUNK)
        o_meta = mla_attend(q_lat[:, :N_META], q_rope[:, :N_META], lat, kr,
                            jnp.broadcast_to(kc[None, :] < 0, (N_META, T)))
        nb = (T - N_META) // Q_BLOCK

        def blocks(a):
            return jnp.swapaxes(a[:, N_META:].reshape((B, nb, Q_BLOCK) + a.shape[2:]), 0, 1)

        def attend_block(args):
            ql, qr, bi = args
            qc = (bi * Q_BLOCK + jnp.arange(Q_BLOCK)) // CHUNK
            return mla_attend(ql, qr, lat, kr, kc[None, :] <= qc[:, None])

        o_fr = lax.map(attend_block, (blocks(q_lat), blocks(q_rope), jnp.arange(nb)))
        o_fr = jnp.swapaxes(o_fr, 0, 1).reshape(B, T - N_META, A_HEADS, KV_LORA)
        o_lat = jnp.concatenate([o_meta, o_fr], axis=1)
    else:
        keys_lat = jnp.concatenate([lat_past.astype(lat.dtype), lat], axis=1)
        keys_kr = jnp.concatenate([kr_past.astype(kr.dtype), kr], axis=1)
        o_lat = mla_attend(q_lat, q_rope, keys_lat, keys_kr, jnp.ones((T, keys_lat.shape[1]), dtype=bool))
    o_v = jnp.einsum('bthr,rhv->bthv', o_lat, w_uv).reshape(B, T, A_HEADS * V_HEAD)
    y_b = o_v @ w_proj_b

    mixed = jax.nn.sigmoid(g_a) * y_a + jax.nn.sigmoid(g_b) * y_b
    x = x + mixed @ w_out

    f = moe(rmsnorm(x, g_ffn).reshape(B * T, D_MODEL), w_router, b_router, w_gu, b_gu, w_dn, b_dn)
    x = x + f.reshape(B, T, D_MODEL)
    return x, (lat, kr, C, n, m, new_buf)


def setup_inputs(seed: int = 0) -> dict:
    key = jax.random.key(seed)
    ks = iter(jax.random.split(key, 40))

    def nrm(shape, scale):
        return jax.random.normal(next(ks), shape, jnp.float32) * scale

    def gain(shape):
        return 1.0 + nrm(shape, 0.01)

    b_if = jnp.concatenate([nrm((DEPTH, M_HEADS), 0.1),
                            jnp.linspace(3.0, 6.0, M_HEADS)[None, :] + nrm((DEPTH, M_HEADS), 0.1)], axis=-1)
    return {
        'x_prompt': nrm((BATCH, SEQ, D_MODEL), 1.0),
        'x_sample': nrm((DEC_BATCH, DEC_SEQ, D_MODEL), 1.0),
        'cache_kv_latent': nrm((DEPTH, DEC_BATCH, PAST_LEN, KV_LORA), 1.0),
        'cache_k_rope': nrm((DEPTH, DEC_BATCH, PAST_LEN, ROPE_DIM), 1.0),
        'state_mlstm_C': nrm((DEPTH, DEC_BATCH, M_HEADS, M_DK, M_DV), 0.02),
        'state_mlstm_n': nrm((DEPTH, DEC_BATCH, M_HEADS, M_DK), 0.1),
        'state_mlstm_m': nrm((DEPTH, DEC_BATCH, M_HEADS), 1.0),
        'state_conv': nrm((DEPTH, DEC_BATCH, CONV_W - 1, 2 * M_QK), 1.0),
        'meta_tokens': nrm((N_META, D_MODEL), 1.0),
        'g_mix_norm': gain((DEPTH, D_MODEL)),
        'w_in': nrm((DEPTH, D_MODEL, D_IN), D_MODEL ** -0.5),
        'b_if': b_if,
        'w_conv': nrm((DEPTH, CONV_W, 2 * M_QK), CONV_W ** -0.5),
        'b_conv': nrm((DEPTH, 2 * M_QK), 0.01),
        'g_mh_norm': gain((DEPTH, M_V)),
        'w_proj_a': nrm((DEPTH, M_V, D_MODEL), M_V ** -0.5),
        'g_q_norm': gain((DEPTH, Q_LORA)),
        'g_kv_norm': gain((DEPTH, KV_LORA)),
        'w_uq': nrm((DEPTH, Q_LORA, A_HEADS * (NOPE_DIM + ROPE_DIM)), Q_LORA ** -0.5),
        'w_uk': nrm((DEPTH, KV_LORA, A_HEADS, NOPE_DIM), KV_LORA ** -0.5),
        'w_uv': nrm((DEPTH, KV_LORA, A_HEADS, V_HEAD), KV_LORA ** -0.5),
        'w_proj_b': nrm((DEPTH, A_HEADS * V_HEAD, D_MODEL), (A_HEADS * V_HEAD) ** -0.5),
        'w_out': nrm((DEPTH, D_MODEL, D_MODEL), D_MODEL ** -0.5),
        'g_ffn_norm': gain((DEPTH, D_MODEL)),
        'w_router': nrm((DEPTH, D_MODEL, N_EXPERTS), D_MODEL ** -0.5),
        'b_router': nrm((DEPTH, N_EXPERTS), 0.01),
        'w_gate_up': nrm((DEPTH, N_EXPERTS, D_MODEL, 2 * D_FF), D_MODEL ** -0.5),
        'b_gate_up': nrm((DEPTH, N_EXPERTS, 2 * D_FF), 0.01),
        'w_down': nrm((DEPTH, N_EXPERTS, D_FF, D_MODEL), D_FF ** -0.5),
        'b_down': nrm((DEPTH, N_EXPERTS, D_MODEL), 0.01),
        'g_final_norm': gain((D_MODEL,)),
    }


def reference(x_prompt, x_sample, cache_kv_latent, cache_k_rope, state_mlstm_C, state_mlstm_n, state_mlstm_m,
              state_conv, meta_tokens, g_mix_norm, w_in, b_if, w_conv, b_conv, g_mh_norm, w_proj_a, g_q_norm,
              g_kv_norm, w_uq, w_uk, w_uv, w_proj_b, w_out, g_ffn_norm, w_router, b_router, w_gate_up, b_gate_up,
              w_down, b_down, g_final_norm):
    B, S = x_prompt.shape[0], x_prompt.shape[1]
    DB, DS = x_sample.shape[0], x_sample.shape[1]
    PL = cache_kv_latent.shape[2]
    xp = jnp.concatenate([jnp.broadcast_to(meta_tokens.astype(x_prompt.dtype)[None], (B, N_META, D_MODEL)),
                          x_prompt], axis=1)
    pos_p = jnp.arange(N_META + S)
    pos_s = N_META + PL + jnp.arange(DS)
    xs = x_sample
    p_states = []
    s_states = []
    for l in range(DEPTH):
        lw = (g_mix_norm[l], w_in[l], b_if[l], w_conv[l], b_conv[l], g_mh_norm[l], w_proj_a[l], g_q_norm[l],
              g_kv_norm[l], w_uq[l], w_uk[l], w_uv[l], w_proj_b[l], w_out[l], g_ffn_norm[l], w_router[l],
              b_router[l], w_gate_up[l], b_gate_up[l], w_down[l], b_down[l])
        xp, ps = trunk_layer(xp, pos_p, jnp.zeros((B, CONV_W - 1, 2 * M_QK), xp.dtype),
                             jnp.zeros((B, M_HEADS, M_DK, M_DV), jnp.float32),
                             jnp.zeros((B, M_HEADS, M_DK), jnp.float32),
                             jnp.zeros((B, M_HEADS), jnp.float32), None, None, True, lw)
        xs, ss = trunk_layer(xs, pos_s, state_conv[l], state_mlstm_C[l], state_mlstm_n[l], state_mlstm_m[l],
                             cache_kv_latent[l], cache_k_rope[l], False, lw)
        p_states.append(ps)
        s_states.append(ss)
    p_lat = jnp.stack([st[0] for st in p_states])
    p_kr = jnp.stack([st[1] for st in p_states])
    p_C = jnp.stack([st[2] for st in p_states])
    p_n = jnp.stack([st[3] for st in p_states])
    p_m = jnp.stack([st[4] for st in p_states])
    p_conv = jnp.stack([st[5] for st in p_states])
    s_lat = jnp.stack([st[0] for st in s_states])
    s_kr = jnp.stack([st[1] for st in s_states])
    s_C = jnp.stack([st[2] for st in s_states])
    s_n = jnp.stack([st[3] for st in s_states])
    s_m = jnp.stack([st[4] for st in s_states])
    s_conv = jnp.stack([st[5] for st in s_states])
    y_prompt = rmsnorm(xp[:, N_META:], g_final_norm)
    y_sample = rmsnorm(xs, g_final_norm)
    return (y_prompt, y_sample, p_lat, p_kr, p_C, p_n, p_m, p_conv, s_lat, s_kr, s_C, s_n, s_m, s_conv)
```

```python
import functools

import jax
import jax.numpy as jnp
import numpy as np
from jax import lax
from jax.experimental import pallas as pl
from jax.experimental.pallas import tpu as pltpu

F32 = jnp.float32
BF16 = jnp.bfloat16
HIGHEST = lax.Precision.HIGHEST

D_MODEL = 1024
N_META = 16
CHUNK = 64
M_HEADS = 4
M_DK = 256
M_DV = 256
M_QK = M_HEADS * M_DK
M_V = M_HEADS * M_DV
CONV_W = 4
A_HEADS = 8
NOPE_DIM = 128
ROPE_DIM = 64
V_HEAD = 128
Q_LORA = 384
KV_LORA = 256
ROPE_THETA = 10000.0
ATTN_SCALE = (NOPE_DIM + ROPE_DIM) ** -0.5
N_EXPERTS = 32
TOP_K = 4
D_FF = 1024
SWIGLU_LIMIT = 7.0
SWIGLU_ALPHA = 1.702
EPS = 1e-6

COL_A = 0
W_A = 2 * M_QK + 2 * M_V
COL_GATE = COL_A + W_A
W_GATE = 128
COL_B = COL_GATE + W_GATE
W_B = Q_LORA + KV_LORA + 2 * ROPE_DIM
COL_G = COL_B + W_B
W_G = 2 * D_MODEL
W_IN_COLS = COL_G + W_G

V7X_VMEM_BYTES = 64 * 2**20
VMEM_LIMIT = V7X_VMEM_BYTES - 8 * 2**20

PREFIX_PAD = 128
MOE_TILE = 512


def _sigmoid(x):
    return 1.0 / (1.0 + jnp.exp(-x))


def _cparams(sem):
    return pltpu.CompilerParams(dimension_semantics=sem, vmem_limit_bytes=VMEM_LIMIT)


def _resident(shape):
    nd = len(shape)
    return pl.BlockSpec(shape, lambda *_: (0,) * nd, pipeline_mode=pl.Buffered(1))


def _inproj_kernel(x_ref, g_ref, w_ref, gq_ref, gkv_ref, ct_ref, st_ref,
                   za_ref, zi_ref, cq_ref, lat_ref, kr_ref, gab_ref, tail_ref, xn_ref, *, seq, spt):
    x = x_ref[...]
    xn = x * lax.rsqrt(jnp.mean(x * x, axis=-1, keepdims=True) + EPS) * g_ref[...]
    xn_ref[...] = xn.astype(BF16)
    xb = xn_ref[...]
    rows = x.shape[0]
    for c in range(W_A // 1024):
        acc = jnp.dot(xb, w_ref[:, COL_A + c * 1024:COL_A + (c + 1) * 1024], preferred_element_type=F32)
        za_ref[:, c * 1024:(c + 1) * 1024] = acc.astype(BF16)
        if c * 1024 < 2 * M_QK:
            if spt == 1:
                tail_ref[0, :, c * 1024:(c + 1) * 1024] = acc[rows - 8:rows, :]
            else:
                for s in range(spt):
                    tail_ref[s, :, c * 1024:(c + 1) * 1024] = acc[(s + 1) * seq - 8:(s + 1) * seq, :]
    zi_ref[...] = jnp.dot(xb, w_ref[:, COL_GATE:COL_GATE + W_GATE], preferred_element_type=F32)
    zb = jnp.dot(xb, w_ref[:, COL_B:COL_B + W_B], preferred_element_type=F32)
    cq = zb[:, 0:Q_LORA]
    cq_ref[...] = (cq * lax.rsqrt(jnp.mean(cq * cq, axis=-1, keepdims=True) + EPS) * gq_ref[...]).astype(BF16)
    ckv = zb[:, Q_LORA:Q_LORA + KV_LORA]
    lat_ref[...] = ckv * lax.rsqrt(jnp.mean(ckv * ckv, axis=-1, keepdims=True) + EPS) * gkv_ref[...]
    k_r = zb[:, Q_LORA + KV_LORA:Q_LORA + KV_LORA + ROPE_DIM]
    k_rs = zb[:, Q_LORA + KV_LORA + ROPE_DIM:W_B]
    kr_ref[...] = k_r * ct_ref[...] + k_rs * st_ref[...]
    for c in range(W_G // 1024):
        acc = jnp.dot(xb, w_ref[:, COL_G + c * 1024:COL_G + (c + 1) * 1024], preferred_element_type=F32)
        gab_ref[:, c * 1024:(c + 1) * 1024] = acc.astype(BF16)


def _inproj(x2d, seq, g_mix, w_cat, g_q, g_kv, ctab, stab):
    m = x2d.shape[0]
    nseq = m // seq
    tm = min(512, m)
    spt = max(1, tm // seq)
    tps = max(1, seq // tm)
    nt = m // tm
    kern = functools.partial(_inproj_kernel, seq=seq, spt=spt)
    row = lambda w: pl.BlockSpec((tm, w), lambda i: (i, 0))
    tab = pl.BlockSpec((tm, ROPE_DIM), lambda i: (i % tps, 0))
    if spt == 1:
        tail_spec = pl.BlockSpec((1, 8, 2 * M_QK), lambda i: (i // tps, 0, 0))
    else:
        tail_spec = pl.BlockSpec((spt, 8, 2 * M_QK), lambda i: (i, 0, 0))
    return pl.pallas_call(
        kern,
        out_shape=(jax.ShapeDtypeStruct((m, W_A), BF16), jax.ShapeDtypeStruct((m, W_GATE), F32),
                   jax.ShapeDtypeStruct((m, Q_LORA), BF16), jax.ShapeDtypeStruct((m, KV_LORA), F32),
                   jax.ShapeDtypeStruct((m, ROPE_DIM), F32), jax.ShapeDtypeStruct((m, W_G), BF16),
                   jax.ShapeDtypeStruct((nseq, 8, 2 * M_QK), F32)),
        grid=(nt,),
        in_specs=[row(D_MODEL), _resident((1, D_MODEL)), _resident((D_MODEL, W_IN_COLS)),
                  _resident((1, Q_LORA)), _resident((1, KV_LORA)), tab, tab],
        out_specs=(row(W_A), row(W_GATE), row(Q_LORA), row(KV_LORA), row(ROPE_DIM), row(W_G), tail_spec),
        scratch_shapes=[pltpu.VMEM((tm, D_MODEL), BF16)],
        compiler_params=_cparams(("arbitrary",)),
        name="inproj",
    )(x2d, g_mix, w_cat, g_q, g_kv, ctab, stab)


def _mlstm_kernel(za_ref, zi_ref, wc_ref, bc_ref, bif_ref, gmh_ref, c0_ref, n0_ref, m0_ref, cv0_ref,
                  hg_ref, cout_ref, nout_ref, mout_ref, cs_ref, ns_ref, ms_ref, xbuf_ref, *, L, nc):
    c = pl.program_id(1)

    @pl.when(c == 0)
    def _():
        cs_ref[...] = c0_ref[...]
        ns_ref[...] = n0_ref[...]
        ms_ref[...] = m0_ref[...]
        xbuf_ref[0:8, :] = cv0_ref[...]

    xbuf_ref[8:8 + L, :] = za_ref[:, 0:2 * M_QK].astype(F32)

    g = zi_ref[...] + bif_ref[...]
    lf = jnp.minimum(g, 0.0) - jnp.log(1.0 + jnp.exp(-jnp.abs(g)))
    row = lax.broadcasted_iota(jnp.int32, (L, L), 0)
    col = lax.broadcasted_iota(jnp.int32, (L, L), 1)
    causal = row >= col
    bcum = jnp.dot(causal.astype(F32), lf, precision=HIGHEST, preferred_element_type=F32)
    sub8 = lax.broadcasted_iota(jnp.int32, (8, 128), 0)
    lane8 = lax.broadcasted_iota(jnp.int32, (8, 128), 1)
    tdims = (((1,), (1,)), ((), ()))
    i_rows = lax.dot_general((lane8 == sub8).astype(F32), g, tdims, precision=HIGHEST, preferred_element_type=F32)
    b_rows = lax.dot_general((lane8 == sub8 + M_HEADS).astype(F32), bcum, tdims, precision=HIGHEST,
                             preferred_element_type=F32)

    def conv_silu(col0):
        acc = bc_ref[:, col0:col0 + M_DK] + xbuf_ref[8:8 + L, col0:col0 + M_DK] * wc_ref[3:4, col0:col0 + M_DK]
        for j in range(CONV_W - 1):
            acc = acc + xbuf_ref[5 + j:5 + j + L, col0:col0 + M_DK] * wc_ref[j:j + 1, col0:col0 + M_DK]
        return acc * _sigmoid(acc)

    for h in range(M_HEADS):
        q = conv_silu(h * M_DK)
        k = conv_silu(M_QK + h * M_DK) * (M_DK ** -0.5)
        v = za_ref[:, 2 * M_QK + h * M_DV:2 * M_QK + (h + 1) * M_DV]
        o = za_ref[:, 2 * M_QK + M_V + h * M_DV:2 * M_QK + M_V + (h + 1) * M_DV].astype(F32)
        i_row = i_rows[h:h + 1, :]
        b_row = b_rows[h:h + 1, :]
        i_col = g[:, h:h + 1]
        b_col = bcum[:, M_HEADS + h:M_HEADS + h + 1]
        m_prev = ms_ref[h][:, 0:1]
        dmat = jnp.where(causal, b_col - b_row + i_row, -jnp.inf)
        inter = b_col + m_prev
        m_t = jnp.maximum(inter, jnp.max(dmat, axis=-1, keepdims=True))
        w_intra = jnp.exp(dmat - m_t)
        w_inter = jnp.exp(inter - m_t)
        qb = q.astype(BF16)
        kb = k.astype(BF16)
        s = lax.dot_general(qb, kb, tdims, preferred_element_type=F32) * w_intra
        c_old = cs_ref[h]
        n_old = ns_ref[h]
        num = (jnp.dot(s.astype(BF16), v, preferred_element_type=F32)
               + w_inter * jnp.dot(qb, c_old.astype(BF16), preferred_element_type=F32))
        qn = jnp.sum(s, axis=-1, keepdims=True) + w_inter * jnp.sum(q * n_old, axis=-1, keepdims=True)
        den = jnp.maximum(jnp.abs(qn), jnp.exp(-m_t))
        hh = num / den
        m_new = m_t[L - 1:L, :]
        b_last = bcum[L - 1:L, M_HEADS + h:M_HEADS + h + 1]
        w_prev = jnp.exp(b_last + m_prev - m_new)
        kw = k * jnp.exp(b_last - b_col + i_col - m_new)
        cs_ref[h] = w_prev * c_old + lax.dot_general(kw.astype(BF16), v, (((0,), (0,)), ((), ())),
                                                     preferred_element_type=F32)
        ns_ref[h] = w_prev * n_old + jnp.sum(kw, axis=0, keepdims=True)
        ms_ref[h] = jnp.broadcast_to(m_new, (1, 128))
        hn = hh * lax.rsqrt(jnp.mean(hh * hh, axis=-1, keepdims=True) + EPS) * gmh_ref[:, h * M_DV:(h + 1) * M_DV]
        hg_ref[:, h * M_DV:(h + 1) * M_DV] = (hn * _sigmoid(o)).astype(BF16)

    xbuf_ref[0:8, :] = xbuf_ref[L:L + 8, :]

    @pl.when(c == nc - 1)
    def _():
        cout_ref[...] = cs_ref[...]
        nout_ref[...] = ns_ref[...]
        mout_ref[...] = ms_ref[...]


def _mlstm(za, zi, seq, w_conv, b_conv, bif, g_mh, c0, n0, m0, cv0):
    b = za.shape[0] // seq
    L = min(seq, 256)
    nc = seq // L
    za3 = za.reshape(b, seq, W_A)
    zi3 = zi.reshape(b, seq, W_GATE)
    shared = c0.shape[0] == 1
    bsel = (lambda bi: 0) if shared else (lambda bi: bi)
    kern = functools.partial(_mlstm_kernel, L=L, nc=nc)
    st4 = lambda last2: pl.BlockSpec((None, M_HEADS) + last2, lambda bi, ci: (bsel(bi), 0, 0, 0))
    out4 = lambda last2: pl.BlockSpec((None, M_HEADS) + last2, lambda bi, ci: (bi, 0, 0, 0))
    return pl.pallas_call(
        kern,
        out_shape=(jax.ShapeDtypeStruct((b, seq, M_V), BF16),
                   jax.ShapeDtypeStruct((b, M_HEADS, M_DK, M_DV), F32),
                   jax.ShapeDtypeStruct((b, M_HEADS, 1, M_DK), F32),
                   jax.ShapeDtypeStruct((b, M_HEADS, 1, 128), F32)),
        grid=(b, nc),
        in_specs=[pl.BlockSpec((None, L, W_A), lambda bi, ci: (bi, ci, 0)),
                  pl.BlockSpec((None, L, W_GATE), lambda bi, ci: (bi, ci, 0)),
                  _resident((CONV_W, 2 * M_QK)), _resident((1, 2 * M_QK)), _resident((1, W_GATE)),
                  _resident((1, M_V)),
                  st4((M_DK, M_DV)), st4((1, M_DK)), st4((1, 128)),
                  pl.BlockSpec((None, 8, 2 * M_QK), lambda bi, ci: (bsel(bi), 0, 0))],
        out_specs=(pl.BlockSpec((None, L, M_V), lambda bi, ci: (bi, ci, 0)),
                   out4((M_DK, M_DV)), out4((1, M_DK)), out4((1, 128))),
        scratch_shapes=[pltpu.VMEM((M_HEADS, M_DK, M_DV), F32), pltpu.VMEM((M_HEADS, 1, M_DK), F32),
                        pltpu.VMEM((M_HEADS, 1, 128), F32), pltpu.VMEM((L + 8, 2 * M_QK), F32)],
        compiler_params=_cparams(("parallel", "arbitrary")),
        name="mlstm",
    )(za3, zi3, w_conv, b_conv, bif, g_mh, c0, n0, m0, cv0)


def _attn_kernel(cq_ref, ct_ref, st_ref, wn_ref, wr_ref, wrs_ref, wuk_ref, wuv_ref,
                 plat_ref, pkr_ref, klat_ref, kkr_ref, o_ref,
                 ql_ref, qr_ref, m_ref, l_ref, acc_ref, *, tq, tk, n_prefix, n_kt, causal):
    i = pl.program_id(1)
    r = A_HEADS * tq
    tdims = (((1,), (1,)), ((), ()))

    cq = cq_ref[...]
    qn_all = jnp.dot(cq, wn_ref[...], preferred_element_type=F32)
    qr_all = jnp.dot(cq, wr_ref[...], preferred_element_type=F32)
    qrs_all = jnp.dot(cq, wrs_ref[...], preferred_element_type=F32)
    ct = ct_ref[...]
    st = st_ref[...]
    for h in range(A_HEADS):
        qn = qn_all[:, h * NOPE_DIM:(h + 1) * NOPE_DIM].astype(BF16)
        ql = jnp.dot(qn, wuk_ref[h], preferred_element_type=F32) * ATTN_SCALE
        ql_ref[h * tq:(h + 1) * tq, :] = ql.astype(BF16)
        qr = (qr_all[:, h * 128:(h + 1) * 128] * ct + qrs_all[:, h * 128:(h + 1) * 128] * st) * ATTN_SCALE
        qr_ref[h * tq:(h + 1) * tq, :] = qr.astype(BF16)

    def scores(kl, kk):
        return (lax.dot_general(ql_ref[...], kl, tdims, preferred_element_type=F32)
                + lax.dot_general(qr_ref[:, 0:ROPE_DIM], kk, tdims, preferred_element_type=F32))

    pl_b = plat_ref[...].astype(BF16)
    s0 = scores(pl_b, pkr_ref[...].astype(BF16))
    pcol = lax.broadcasted_iota(jnp.int32, s0.shape, 1)
    s0 = jnp.where(pcol < n_prefix, s0, -jnp.inf)
    m0 = jnp.max(s0, axis=-1, keepdims=True)
    p0 = jnp.exp(s0 - m0)
    m_ref[...] = m0
    l_ref[...] = jnp.sum(p0, axis=-1, keepdims=True)
    acc_ref[...] = jnp.dot(p0.astype(BF16), pl_b, preferred_element_type=F32)

    def kv_step(j, masked):
        start = pl.multiple_of(j * tk, tk)
        kl = klat_ref[pl.ds(start, tk), :].astype(BF16)
        kk = kkr_ref[pl.ds(start, tk), :].astype(BF16)
        s = scores(kl, kk)
        if masked:
            qpos = i * tq + (lax.broadcasted_iota(jnp.int32, s.shape, 0) & (tq - 1))
            kpos = start + lax.broadcasted_iota(jnp.int32, s.shape, 1)
            shift = CHUNK.bit_length() - 1
            s = jnp.where(jnp.right_shift(kpos, shift) <= jnp.right_shift(qpos, shift), s, -jnp.inf)
        m_old = m_ref[...]
        m_new = jnp.maximum(m_old, jnp.max(s, axis=-1, keepdims=True))
        alpha = jnp.exp(m_old - m_new)
        p = jnp.exp(s - m_new)
        l_ref[...] = alpha * l_ref[...] + jnp.sum(p, axis=-1, keepdims=True)
        acc_ref[...] = alpha * acc_ref[...] + jnp.dot(p.astype(BF16), kl, preferred_element_type=F32)
        m_ref[...] = m_new

    if causal:
        last = (i * tq) // tk

        def body(j, carry):
            kv_step(j, False)
            return carry

        lax.fori_loop(0, last, body, 0)
        kv_step(last, True)
    else:
        def body(j, carry):
            kv_step(j, False)
            return carry

        lax.fori_loop(0, n_kt, body, 0)

    o = acc_ref[...] / l_ref[...]
    for h in range(A_HEADS):
        oh = o[h * tq:(h + 1) * tq, :].astype(BF16)
        o_ref[:, h * V_HEAD:(h + 1) * V_HEAD] = jnp.dot(oh, wuv_ref[h], preferred_element_type=F32).astype(BF16)


def _attention(cqn, seq, ctab, stab, wn, wr, wrs, wuk, wuv, plat, pkr, n_prefix, klat, kkr, causal):
    b = cqn.shape[0] // seq
    tq = min(seq, 128)
    nq = seq // tq
    tkeys = klat.shape[1]
    tk = 256
    n_kt = tkeys // tk
    r = A_HEADS * tq
    cq3 = cqn.reshape(b, seq, Q_LORA)
    psel = (lambda bi: 0) if plat.shape[0] == 1 else (lambda bi: bi)
    kern = functools.partial(_attn_kernel, tq=tq, tk=tk, n_prefix=n_prefix, n_kt=n_kt, causal=causal)
    tab = pl.BlockSpec((tq, 128), lambda bi, qi: (qi, 0))
    out = pl.pallas_call(
        kern,
        out_shape=jax.ShapeDtypeStruct((b, seq, A_HEADS * V_HEAD), BF16),
        grid=(b, nq),
        in_specs=[pl.BlockSpec((None, tq, Q_LORA), lambda bi, qi: (bi, qi, 0)), tab, tab,
                  _resident((Q_LORA, A_HEADS * NOPE_DIM)), _resident((Q_LORA, A_HEADS * 128)),
                  _resident((Q_LORA, A_HEADS * 128)), _resident((A_HEADS, NOPE_DIM, KV_LORA)),
                  _resident((A_HEADS, KV_LORA, V_HEAD)),
                  pl.BlockSpec((None, PREFIX_PAD, KV_LORA), lambda bi, qi: (psel(bi), 0, 0)),
                  pl.BlockSpec((None, PREFIX_PAD, ROPE_DIM), lambda bi, qi: (psel(bi), 0, 0)),
                  pl.BlockSpec((None, tkeys, KV_LORA), lambda bi, qi: (bi, 0, 0)),
                  pl.BlockSpec((None, tkeys, ROPE_DIM), lambda bi, qi: (bi, 0, 0))],
        out_specs=pl.BlockSpec((None, tq, A_HEADS * V_HEAD), lambda bi, qi: (bi, qi, 0)),
        scratch_shapes=[pltpu.VMEM((r, KV_LORA), BF16), pltpu.VMEM((r, 128), BF16),
                        pltpu.VMEM((r, 1), F32), pltpu.VMEM((r, 1), F32), pltpu.VMEM((r, KV_LORA), F32)],
        compiler_params=_cparams(("parallel", "arbitrary")),
        name="attention",
    )(cq3, ctab, stab, wn, wr, wrs, wuk, wuv, plat, pkr, klat, kkr)
    return out.reshape(b * seq, A_HEADS * V_HEAD)


def _merge_kernel(hg_ref, ov_ref, gab_ref, x_ref, wa_ref, wb_ref, wo_ref, gf_ref, wr_ref, br_ref,
                  x1_ref, x2_ref, ti_ref, tg_ref, tr_ref, cnt_ref, carry_ref):
    i = pl.program_id(0)

    @pl.when(i == 0)
    def _():
        carry_ref[...] = jnp.zeros_like(carry_ref)

    tm = x_ref.shape[0]
    ya = jnp.dot(hg_ref[...], wa_ref[...], preferred_element_type=F32)
    yb = jnp.dot(ov_ref[...], wb_ref[...], preferred_element_type=F32)
    mixed = (_sigmoid(gab_ref[:, 0:D_MODEL].astype(F32)) * ya
             + _sigmoid(gab_ref[:, D_MODEL:2 * D_MODEL].astype(F32)) * yb)
    x1 = x_ref[...] + jnp.dot(mixed.astype(BF16), wo_ref[...], preferred_element_type=F32)
    x1_ref[...] = x1
    x2 = x1 * lax.rsqrt(jnp.mean(x1 * x1, axis=-1, keepdims=True) + EPS) * gf_ref[...]
    x2_ref[...] = x2

    lane = lax.broadcasted_iota(jnp.int32, (tm, 128), 1)
    lane_f = lane.astype(F32)
    logits = jnp.dot(x2, wr_ref[...], precision=HIGHEST, preferred_element_type=F32) + br_ref[...]
    cur = jnp.where(lane < N_EXPERTS, logits, -jnp.inf)
    vals, idxs = [], []
    for _ in range(TOP_K):
        mx = jnp.max(cur, axis=-1, keepdims=True)
        idx = jnp.min(jnp.where(cur == mx, lane_f, 128.0), axis=-1, keepdims=True)
        vals.append(mx)
        idxs.append(idx)
        cur = jnp.where(lane_f == idx, -jnp.inf, cur)
    es = [jnp.exp(v - vals[0]) for v in vals]
    tot = es[0] + es[1] + es[2] + es[3]

    onehots = [(lane_f == idx) for idx in idxs]
    cnt = jnp.zeros((tm, 128), F32)
    for oh in onehots:
        cnt = cnt + jnp.where(oh, 1.0, 0.0)
    rr = lax.broadcasted_iota(jnp.int32, (tm, tm), 0)
    cc = lax.broadcasted_iota(jnp.int32, (tm, tm), 1)
    before = jnp.dot((rr > cc).astype(BF16), cnt.astype(BF16), preferred_element_type=F32) + carry_ref[0:1, :]
    ti = jnp.zeros((tm, 128), F32)
    tg = jnp.zeros((tm, 128), F32)
    tr = jnp.zeros((tm, 128), F32)
    for k in range(TOP_K):
        rank = jnp.sum(jnp.where(onehots[k], before, 0.0), axis=-1, keepdims=True)
        ti = jnp.where(lane == k, idxs[k], ti)
        tg = jnp.where(lane == k, es[k] / tot, tg)
        tr = jnp.where(lane == k, rank, tr)
    ti_ref[...] = ti.astype(jnp.int32)
    tg_ref[...] = tg
    tr_ref[...] = tr.astype(jnp.int32)
    new_carry = carry_ref[0:1, :] + jnp.sum(cnt, axis=0, keepdims=True)
    carry_ref[...] = jnp.broadcast_to(new_carry, carry_ref.shape)
    cnt_ref[...] = jnp.broadcast_to(new_carry, cnt_ref.shape).astype(jnp.int32)


def _merge_route(hg, ov, gab, x2d, wa, wb, wo, g_ffn, w_router, b_router):
    m = x2d.shape[0]
    tm = min(512, m)
    row = lambda w: pl.BlockSpec((tm, w), lambda i: (i, 0))
    sq = _resident((D_MODEL, D_MODEL))
    return pl.pallas_call(
        _merge_kernel,
        out_shape=(jax.ShapeDtypeStruct((m, D_MODEL), F32), jax.ShapeDtypeStruct((m, D_MODEL), F32),
                   jax.ShapeDtypeStruct((m, 128), jnp.int32), jax.ShapeDtypeStruct((m, 128), F32),
                   jax.ShapeDtypeStruct((m, 128), jnp.int32), jax.ShapeDtypeStruct((8, 128), jnp.int32)),
        grid=(m // tm,),
        in_specs=[row(M_V), row(A_HEADS * V_HEAD), row(W_G), row(D_MODEL), sq, sq, sq,
                  _resident((1, D_MODEL)), _resident((D_MODEL, 128)), _resident((1, 128))],
        out_specs=(row(D_MODEL), row(D_MODEL), row(128), row(128), row(128),
                   pl.BlockSpec((8, 128), lambda i: (0, 0))),
        scratch_shapes=[pltpu.VMEM((8, 128), F32)],
        compiler_params=_cparams(("arbitrary",)),
        name="merge_route",
    )(hg, ov, gab, x2d, wa, wb, wo, g_ffn, w_router, b_router)


def _row_copy(src_ref, src_row, dst_ref, dst_row, sem):
    return pltpu.make_async_copy(src_ref.at[pl.ds(src_row, 1), :], dst_ref.at[pl.ds(dst_row, 1), :], sem)


def _dispatch_kernel(dest_ref, x_ref, xs_in_ref, xs_ref, sem):
    del xs_in_ref
    tr = x_ref.shape[0]

    def issue(r, carry):
        for k in range(TOP_K):
            _row_copy(x_ref, r, xs_ref, dest_ref[0, 0, r * TOP_K + k], sem).start()
        return carry

    lax.fori_loop(0, tr, issue, 0, unroll=8)
    for _ in range(TOP_K):
        pltpu.make_async_copy(x_ref, xs_ref.at[pl.ds(0, tr), :], sem).wait()


def _dispatch(x2, dest, n_rows):
    m, w = x2.shape
    tr = min(256, m)
    nt = m // tr
    dest3 = dest.reshape(nt, 1, tr * TOP_K)
    xs0 = jnp.zeros((n_rows, w), x2.dtype)
    return pl.pallas_call(
        _dispatch_kernel,
        out_shape=jax.ShapeDtypeStruct((n_rows, w), x2.dtype),
        grid=(nt,),
        in_specs=[pl.BlockSpec((1, 1, tr * TOP_K), lambda i: (i, 0, 0), memory_space=pltpu.SMEM),
                  pl.BlockSpec((tr, w), lambda i: (i, 0)),
                  pl.BlockSpec(memory_space=pl.ANY)],
        out_specs=pl.BlockSpec(memory_space=pl.ANY),
        scratch_shapes=[pltpu.SemaphoreType.DMA(())],
        input_output_aliases={2: 0},
        compiler_params=_cparams(("arbitrary",)),
        name="moe_dispatch",
    )(dest3, x2, xs0)


def _expert_kernel(te_ref, nu_ref, xs_ref, wgu_ref, bgu_ref, wdn_ref, bdn_ref, ys_ref, wgu_b, wdn_b):
    i = pl.program_id(0)
    prev = te_ref[jnp.maximum(i - 1, 0)]

    @pl.when(i < nu_ref[0])
    def _():
        @pl.when((i == 0) | (te_ref[i] != prev))
        def _():
            rc = 128
            for c in range(D_MODEL // rc):
                wgu_b[c * rc:(c + 1) * rc, :] = wgu_ref[c * rc:(c + 1) * rc, :].astype(BF16)
                wdn_b[c * rc:(c + 1) * rc, :] = wdn_ref[c * rc:(c + 1) * rc, :].astype(BF16)

        xb = xs_ref[...].astype(BF16)
        fc = 512
        y = jnp.zeros((xs_ref.shape[0], D_MODEL), F32) + bdn_ref[...]
        for f0 in range(0, D_FF, fc):
            def proj(c0):
                return jnp.dot(xb, wgu_b[:, c0:c0 + fc], preferred_element_type=F32) + bgu_ref[:, c0:c0 + fc]
            gate = jnp.minimum(proj(f0), SWIGLU_LIMIT)
            up = jnp.clip(proj(D_FF + f0), -SWIGLU_LIMIT, SWIGLU_LIMIT)
            hid = (up + 1.0) * gate * _sigmoid(SWIGLU_ALPHA * gate)
            y = y + jnp.dot(hid.astype(BF16), wdn_b[f0:f0 + fc, :], preferred_element_type=F32)
        ys_ref[...] = y

    @pl.when(i >= nu_ref[0])
    def _():
        ys_ref[...] = jnp.zeros_like(ys_ref)


def _experts(xs, tile_expert, n_used, w_gu, b_gu, w_dn, b_dn):
    n_rows, w = xs.shape
    tm = MOE_TILE
    nt = n_rows // tm
    tile = lambda i, te, nu: (jnp.minimum(i, nu[0] - 1), 0)
    grid_spec = pltpu.PrefetchScalarGridSpec(
        num_scalar_prefetch=2,
        grid=(nt,),
        in_specs=[pl.BlockSpec((tm, w), tile),
                  pl.BlockSpec((None, D_MODEL, 2 * D_FF), lambda i, te, nu: (te[i], 0, 0)),
                  pl.BlockSpec((None, 1, 2 * D_FF), lambda i, te, nu: (te[i], 0, 0)),
                  pl.BlockSpec((None, D_FF, D_MODEL), lambda i, te, nu: (te[i], 0, 0)),
                  pl.BlockSpec((None, 1, D_MODEL), lambda i, te, nu: (te[i], 0, 0))],
        out_specs=pl.BlockSpec((tm, w), lambda i, te, nu: (i, 0)),
        scratch_shapes=[pltpu.VMEM((D_MODEL, 2 * D_FF), BF16), pltpu.VMEM((D_FF, D_MODEL), BF16)],
    )
    return pl.pallas_call(
        _expert_kernel,
        out_shape=jax.ShapeDtypeStruct((n_rows, w), F32),
        grid_spec=grid_spec,
        compiler_params=_cparams(("arbitrary",)),
        name="moe_experts",
    )(tile_expert, n_used, xs, w_gu, b_gu, w_dn, b_dn)


def _combine_kernel(dcur_ref, dnext_ref, x1_ref, tg_ref, gfin_ref, ys_ref, y_ref, buf_ref, sem_ref, *, nt):
    i = pl.program_id(0)
    tc = x1_ref.shape[0]

    def issue(dref, slot):
        def body(r, carry):
            for k in range(TOP_K):
                pltpu.make_async_copy(ys_ref.at[pl.ds(dref[0, 0, r * TOP_K + k], 1), :],
                                      buf_ref.at[slot, k, pl.ds(r, 1), :], sem_ref.at[slot]).start()
            return carry
        lax.fori_loop(0, tc, body, 0, unroll=8)

    slot = i % 2

    @pl.when(i == 0)
    def _():
        issue(dcur_ref, 0)

    @pl.when(i + 1 < nt)
    def _():
        issue(dnext_ref, 1 - slot)

    for k in range(TOP_K):
        pltpu.make_async_copy(ys_ref.at[pl.ds(0, tc), :], buf_ref.at[slot, k], sem_ref.at[slot]).wait()

    x = x1_ref[...]
    for k in range(TOP_K):
        x = x + tg_ref[:, k:k + 1] * buf_ref[slot, k]
    y_ref[...] = x * lax.rsqrt(jnp.mean(x * x, axis=-1, keepdims=True) + EPS) * gfin_ref[...]


def _combine(dest, x1, tg, g_final, ys):
    m = x1.shape[0]
    tc = min(256, m)
    nt = m // tc
    dest3 = dest.reshape(nt, 1, tc * TOP_K)
    kern = functools.partial(_combine_kernel, nt=nt)
    smem = lambda f: pl.BlockSpec((1, 1, tc * TOP_K), f, memory_space=pltpu.SMEM)
    return pl.pallas_call(
        kern,
        out_shape=jax.ShapeDtypeStruct((m, D_MODEL), F32),
        grid=(nt,),
        in_specs=[smem(lambda i: (i, 0, 0)), smem(lambda i: (jnp.minimum(i + 1, nt - 1), 0, 0)),
                  pl.BlockSpec((tc, D_MODEL), lambda i: (i, 0)), pl.BlockSpec((tc, 128), lambda i: (i, 0)),
                  _resident((1, D_MODEL)), pl.BlockSpec(memory_space=pl.ANY)],
        out_specs=pl.BlockSpec((tc, D_MODEL), lambda i: (i, 0)),
        scratch_shapes=[pltpu.VMEM((2, TOP_K, tc, D_MODEL), F32), pltpu.SemaphoreType.DMA((2,))],
        compiler_params=_cparams(("arbitrary",)),
        name="moe_combine",
    )(dest3, dest3, x1, tg, g_final, ys)


def _rope_tables(pos):
    half = ROPE_DIM // 2
    freqs = ROPE_THETA ** (-jnp.arange(half, dtype=F32) / half)
    ang = pos.astype(F32)[:, None] * freqs[None, :]
    cos, sin = jnp.cos(ang), jnp.sin(ang)
    return jnp.concatenate([cos, cos], axis=-1), jnp.concatenate([-sin, sin], axis=-1)


def _tile_rows(t, rows):
    return t if t.shape[0] >= rows else jnp.tile(t, (rows // t.shape[0], 1))


def _pad_lanes(t, width):
    return jnp.pad(t, ((0, 0), (0, width - t.shape[1])))


def _moe(x1, x2n, ti, tg, tr, cnt, g_final, w_gu, b_gu, w_dn, b_dn):
    m = x1.shape[0]
    a = m * TOP_K
    tm = MOE_TILE
    n_tiles = -(-a // tm) + N_EXPERTS
    counts = cnt[0, :N_EXPERTS]
    pcounts = (counts + tm - 1) // tm * tm
    pend = jnp.cumsum(pcounts)
    pstart = pend - pcounts
    top_i = ti[:, :TOP_K]
    eids = jnp.arange(N_EXPERTS, dtype=jnp.int32)
    dest = tr[:, :TOP_K] + jnp.sum(jnp.where(top_i[..., None] == eids, pstart, 0), axis=-1)
    n_used = (pend[-1] // tm).astype(jnp.int32)
    tiles = jnp.minimum(jnp.arange(n_tiles, dtype=jnp.int32), n_used - 1) * tm
    tile_expert = jnp.minimum(jnp.sum(pend[None, :] <= tiles[:, None], axis=-1), N_EXPERTS - 1).astype(jnp.int32)
    xs = _dispatch(x2n, dest.astype(jnp.int32), n_tiles * tm)
    ys = _experts(xs, tile_expert, n_used.reshape(1), w_gu, b_gu, w_dn, b_dn)
    return _combine(dest.astype(jnp.int32), x1, tg, g_final, ys)


def kernel(x_prompt, x_sample, cache_kv_latent, cache_k_rope, state_mlstm_C, state_mlstm_n, state_mlstm_m, state_conv, meta_tokens, g_mix_norm, w_in, b_if, w_conv, b_conv, g_mh_norm, w_proj_a, g_q_norm, g_kv_norm, w_uq, w_uk, w_uv, w_proj_b, w_out, g_ffn_norm, w_router, b_router, w_gate_up, b_gate_up, w_down, b_down, g_final_norm):
    bsz, seq = x_prompt.shape[0], x_prompt.shape[1]
    dbs, dseq = x_sample.shape[0], x_sample.shape[1]
    past = cache_kv_latent.shape[2]
    assert w_in.shape[0] == 1, "single-layer trunk"
    l = 0

    wi = w_in[l]
    o_gate = 2 * M_QK + 2 * M_V
    o_cq = o_gate + 2 * M_HEADS
    o_ckv = o_cq + Q_LORA
    o_kr = o_ckv + KV_LORA
    o_g = o_kr + ROPE_DIM
    swap = np.concatenate([np.arange(ROPE_DIM // 2, ROPE_DIM), np.arange(ROPE_DIM // 2)])
    w_kr = wi[:, o_kr:o_g]
    w_cat = jnp.concatenate([wi[:, :o_gate], _pad_lanes(wi[:, o_gate:o_cq], W_GATE), wi[:, o_cq:o_kr], w_kr,
                             w_kr[:, swap], wi[:, o_g:]], axis=1).astype(BF16)
    bif = _pad_lanes(b_if[l][None, :], W_GATE)
    uq = w_uq[l].reshape(Q_LORA, A_HEADS, NOPE_DIM + ROPE_DIM)
    wn = uq[:, :, :NOPE_DIM].reshape(Q_LORA, A_HEADS * NOPE_DIM).astype(BF16)
    uq_r = uq[:, :, NOPE_DIM:]
    pad_r = lambda t: jnp.pad(t, ((0, 0), (0, 0), (0, 128 - ROPE_DIM))).reshape(Q_LORA, A_HEADS * 128).astype(BF16)
    wr, wrs = pad_r(uq_r), pad_r(uq_r[:, :, swap])
    wuk = jnp.transpose(w_uk[l], (1, 2, 0)).astype(BF16)
    wuv = jnp.transpose(w_uv[l], (1, 0, 2)).astype(BF16)
    wa, wb, wo = w_proj_a[l].astype(BF16), w_proj_b[l].astype(BF16), w_out[l].astype(BF16)
    w_rt = _pad_lanes(w_router[l], 128)
    b_rt = _pad_lanes(b_router[l][None, :], 128)
    g_mix, g_q, g_kv = g_mix_norm[l][None, :], g_q_norm[l][None, :], g_kv_norm[l][None, :]
    g_mh, g_ffn, g_fin = g_mh_norm[l][None, :], g_ffn_norm[l][None, :], g_final_norm[None, :]
    wcv, bcv = w_conv[l], b_conv[l][None, :]
    w_gu, b_gu = w_gate_up[l], b_gate_up[l][:, None, :]
    w_dn, b_dn = w_down[l], b_down[l][:, None, :]

    ct_m, st_m = _rope_tables(jnp.arange(N_META))
    ct_p, st_p = _rope_tables(N_META + jnp.arange(seq))
    ct_s, st_s = _rope_tables(N_META + past + jnp.arange(dseq))

    def stream_tables(ct, st, m_rows):
        rows = max(ct.shape[0], min(512, m_rows))
        return _tile_rows(ct, rows), _tile_rows(st, rows)

    def prefix(lat, kr):
        pad = ((0, 0), (0, PREFIX_PAD - lat.shape[1]), (0, 0))
        return jnp.pad(lat, pad), jnp.pad(kr, pad)

    za_m, zi_m, _, lat_m, kr_m, _, tail_m = _inproj(meta_tokens, N_META, g_mix, w_cat, g_q, g_kv, ct_m, st_m)
    npad = CHUNK - N_META
    za_mp = jnp.concatenate([jnp.zeros((npad, W_A), BF16), za_m], axis=0)
    lane = jnp.arange(W_GATE)
    neutral = jnp.where(lane < M_HEADS, -1e30, jnp.where(lane < 2 * M_HEADS, 1e30, 0.0)).astype(F32)
    zi_mp = jnp.concatenate([jnp.broadcast_to(neutral, (npad, W_GATE)), zi_m], axis=0)
    zeros_state = (jnp.zeros((1, M_HEADS, M_DK, M_DV), F32), jnp.zeros((1, M_HEADS, 1, M_DK), F32),
                   jnp.zeros((1, M_HEADS, 1, 128), F32), jnp.zeros((1, 8, 2 * M_QK), F32))
    _, c_m, n_m, m_m = _mlstm(za_mp, zi_mp, CHUNK, wcv, bcv, bif, g_mh, *zeros_state)

    def layer_tail(x2d, hg, ov, gab):
        x1, x2n, ti, tg, tr, cnt = _merge_route(hg.reshape(x2d.shape[0], M_V), ov, gab, x2d, wa, wb, wo, g_ffn,
                                                w_rt, b_rt)
        return _moe(x1, x2n, ti, tg, tr, cnt, g_fin, w_gu, b_gu, w_dn, b_dn)

    xp = x_prompt.reshape(bsz * seq, D_MODEL)
    ctp, stp = stream_tables(ct_p, st_p, bsz * seq)
    za, zi, cqn, lat, kr, gab, tail = _inproj(xp, seq, g_mix, w_cat, g_q, g_kv, ctp, stp)
    hg, c_p, n_p, m_p = _mlstm(za, zi, seq, wcv, bcv, bif, g_mh, c_m, n_m, m_m, tail_m)
    plat, pkr = prefix(lat_m[None], kr_m[None])
    ctq, stq = _pad_lanes(ct_p, 128), _pad_lanes(st_p, 128)
    ov = _attention(cqn, seq, ctq, stq, wn, wr, wrs, wuk, wuv, plat, pkr, N_META,
                    lat.reshape(bsz, seq, KV_LORA), kr.reshape(bsz, seq, ROPE_DIM), True)
    y_prompt = layer_tail(xp, hg, ov, gab).reshape(bsz, seq, D_MODEL)

    xs2 = x_sample.reshape(dbs * dseq, D_MODEL)
    cts, sts = stream_tables(ct_s, st_s, dbs * dseq)
    za, zi, cqn, lat_s, kr_s, gab, tail_s = _inproj(xs2, dseq, g_mix, w_cat, g_q, g_kv, cts, sts)
    cv0 = jnp.pad(state_conv[l], ((0, 0), (8 - (CONV_W - 1), 0), (0, 0)))
    m0 = jnp.broadcast_to(state_mlstm_m[l][:, :, None, None], (dbs, M_HEADS, 1, 128))
    hg, c_s, n_s, m_s = _mlstm(za, zi, dseq, wcv, bcv, bif, g_mh, state_mlstm_C[l],
                               state_mlstm_n[l][:, :, None, :], m0, cv0)
    plat, pkr = prefix(lat_s.reshape(dbs, dseq, KV_LORA), kr_s.reshape(dbs, dseq, ROPE_DIM))
    ov = _attention(cqn, dseq, _pad_lanes(ct_s, 128), _pad_lanes(st_s, 128), wn, wr, wrs, wuk, wuv, plat, pkr,
                    dseq, cache_kv_latent[l], cache_k_rope[l], False)
    y_sample = layer_tail(xs2, hg, ov, gab).reshape(dbs, dseq, D_MODEL)

    def with_meta(meta_rows, frames, width):
        return jnp.concatenate([jnp.broadcast_to(meta_rows[None], (bsz, N_META, width)),
                                frames.reshape(bsz, seq, width)], axis=1)[None]

    p_lat = with_meta(lat_m, lat, KV_LORA)
    p_kr = with_meta(kr_m, kr, ROPE_DIM)
    tail3 = lambda t: t[:, 8 - (CONV_W - 1):, :][None]
    return (y_prompt, y_sample, p_lat, p_kr, c_p[None], n_p[:, :, 0, :][None], m_p[:, :, 0, 0][None], tail3(tail),
            lat_s.reshape(dbs, dseq, KV_LORA)[None], kr_s.reshape(dbs, dseq, ROPE_DIM)[None], c_s[None],
            n_s[:, :, 0, :][None], m_s[:, :, 0, 0][None], tail3(tail_s))
```

```python
import functools

import jax
import jax.numpy as jnp
import numpy as np
from jax import lax
from jax.experimental import pallas as pl
from jax.experimental.pallas import tpu as pltpu

F32 = jnp.float32
BF16 = jnp.bfloat16
HIGHEST = lax.Precision.HIGHEST

D_MODEL = 1024
N_META = 16
CHUNK = 64
M_HEADS = 4
M_DK = 256
M_DV = 256
M_QK = M_HEADS * M_DK
M_V = M_HEADS * M_DV
CONV_W = 4
A_HEADS = 8
NOPE_DIM = 128
ROPE_DIM = 64
V_HEAD = 128
Q_LORA = 384
KV_LORA = 256
ROPE_THETA = 10000.0
ATTN_SCALE = (NOPE_DIM + ROPE_DIM) ** -0.5
N_EXPERTS = 32
TOP_K = 4
D_FF = 1024
SWIGLU_LIMIT = 7.0
SWIGLU_ALPHA = 1.702
EPS = 1e-6

COL_A = 0
W_A = 2 * M_QK + 2 * M_V
COL_GATE = COL_A + W_A
W_GATE = 128
COL_B = COL_GATE + W_GATE
W_B = Q_LORA + KV_LORA + 2 * ROPE_DIM
COL_G = COL_B + W_B
W_G = 2 * D_MODEL
W_IN_COLS = COL_G + W_G

V7X_VMEM_BYTES = 64 * 2**20
VMEM_LIMIT = V7X_VMEM_BYTES - 8 * 2**20

PREFIX_PAD = 128
MOE_TILE = 512


def _sigmoid(x):
    return 1.0 / (1.0 + jnp.exp(-x))


def _cparams(sem):
    return pltpu.CompilerParams(dimension_semantics=sem, vmem_limit_bytes=VMEM_LIMIT)


def _resident(shape):
    nd = len(shape)
    return pl.BlockSpec(shape, lambda *_: (0,) * nd, pipeline_mode=pl.Buffered(1))


def _inproj_kernel(x_ref, g_ref, w_ref, gq_ref, gkv_ref, ct_ref, st_ref,
                   za_ref, zi_ref, cq_ref, lat_ref, kr_ref, gab_ref, tail_ref, xn_ref, *, seq, spt):
    x = x_ref[...]
    xn = x * lax.rsqrt(jnp.mean(x * x, axis=-1, keepdims=True) + EPS) * g_ref[...]
    xn_ref[...] = xn.astype(BF16)
    xb = xn_ref[...]
    rows = x.shape[0]
    for c in range(W_A // 1024):
        acc = jnp.dot(xb, w_ref[:, COL_A + c * 1024:COL_A + (c + 1) * 1024], preferred_element_type=F32)
        za_ref[:, c * 1024:(c + 1) * 1024] = acc.astype(BF16)
        if c * 1024 < 2 * M_QK:
            if spt == 1:
                tail_ref[0, :, c * 1024:(c + 1) * 1024] = acc[rows - 8:rows, :]
            else:
                for s in range(spt):
                    tail_ref[s, :, c * 1024:(c + 1) * 1024] = acc[(s + 1) * seq - 8:(s + 1) * seq, :]
    zi_ref[...] = jnp.dot(xb, w_ref[:, COL_GATE:COL_GATE + W_GATE], preferred_element_type=F32)
    zb = jnp.dot(xb, w_ref[:, COL_B:COL_B + W_B], preferred_element_type=F32)
    cq = zb[:, 0:Q_LORA]
    cq_ref[...] = (cq * lax.rsqrt(jnp.mean(cq * cq, axis=-1, keepdims=True) + EPS) * gq_ref[...]).astype(BF16)
    ckv = zb[:, Q_LORA:Q_LORA + KV_LORA]
    lat_ref[...] = ckv * lax.rsqrt(jnp.mean(ckv * ckv, axis=-1, keepdims=True) + EPS) * gkv_ref[...]
    k_r = zb[:, Q_LORA + KV_LORA:Q_LORA + KV_LORA + ROPE_DIM]
    k_rs = zb[:, Q_LORA + KV_LORA + ROPE_DIM:W_B]
    kr_ref[...] = k_r * ct_ref[...] + k_rs * st_ref[...]
    for c in range(W_G // 1024):
        acc = jnp.dot(xb, w_ref[:, COL_G + c * 1024:COL_G + (c + 1) * 1024], preferred_element_type=F32)
        gab_ref[:, c * 1024:(c + 1) * 1024] = acc.astype(BF16)


def _inproj(x2d, seq, g_mix, w_cat, g_q, g_kv, ctab, stab):
    m = x2d.shape[0]
    nseq = m // seq
    tm = min(512, m)
    spt = max(1, tm // seq)
    tps = max(1, seq // tm)
    nt = m // tm
    kern = functools.partial(_inproj_kernel, seq=seq, spt=spt)
    row = lambda w: pl.BlockSpec((tm, w), lambda i: (i, 0))
    tab = pl.BlockSpec((tm, ROPE_DIM), lambda i: (i % tps, 0))
    if spt == 1:
        tail_spec = pl.BlockSpec((1, 8, 2 * M_QK), lambda i: (i // tps, 0, 0))
    else:
        tail_spec = pl.BlockSpec((spt, 8, 2 * M_QK), lambda i: (i, 0, 0))
    return pl.pallas_call(
        kern,
        out_shape=(jax.ShapeDtypeStruct((m, W_A), BF16), jax.ShapeDtypeStruct((m, W_GATE), F32),
                   jax.ShapeDtypeStruct((m, Q_LORA), BF16), jax.ShapeDtypeStruct((m, KV_LORA), F32),
                   jax.ShapeDtypeStruct((m, ROPE_DIM), F32), jax.ShapeDtypeStruct((m, W_G), BF16),
                   jax.ShapeDtypeStruct((nseq, 8, 2 * M_QK), F32)),
        grid=(nt,),
        in_specs=[row(D_MODEL), _resident((1, D_MODEL)), _resident((D_MODEL, W_IN_COLS)),
                  _resident((1, Q_LORA)), _resident((1, KV_LORA)), tab, tab],
        out_specs=(row(W_A), row(W_GATE), row(Q_LORA), row(KV_LORA), row(ROPE_DIM), row(W_G), tail_spec),
        scratch_shapes=[pltpu.VMEM((tm, D_MODEL), BF16)],
        compiler_params=_cparams(("arbitrary",)),
        name="inproj",
    )(x2d, g_mix, w_cat, g_q, g_kv, ctab, stab)


def _mlstm_kernel(za_ref, zi_ref, wc_ref, bc_ref, bif_ref, gmh_ref, c0_ref, n0_ref, m0_ref, cv0_ref,
                  hg_ref, cout_ref, nout_ref, mout_ref, cs_ref, ns_ref, ms_ref, xbuf_ref, *, L, nc):
    c = pl.program_id(1)

    @pl.when(c == 0)
    def _():
        cs_ref[...] = c0_ref[...]
        ns_ref[...] = n0_ref[...]
        ms_ref[...] = m0_ref[...]
        xbuf_ref[0:8, :] = cv0_ref[...]

    xbuf_ref[8:8 + L, :] = za_ref[:, 0:2 * M_QK].astype(F32)

    g = zi_ref[...] + bif_ref[...]
    lf = jnp.minimum(g, 0.0) - jnp.log(1.0 + jnp.exp(-jnp.abs(g)))
    row = lax.broadcasted_iota(jnp.int32, (L, L), 0)
    col = lax.broadcasted_iota(jnp.int32, (L, L), 1)
    causal = row >= col
    tdims = (((1,), (1,)), ((), ()))

    def dot01(a01, x, dims):
        a = a01.astype(BF16)
        x_hi = x.astype(BF16)
        x_lo = (x - x_hi.astype(F32)).astype(BF16)
        return (lax.dot_general(a, x_hi, dims, preferred_element_type=F32)
                + lax.dot_general(a, x_lo, dims, preferred_element_type=F32))

    bcum = dot01(causal, lf, (((1,), (0,)), ((), ())))
    sub8 = lax.broadcasted_iota(jnp.int32, (8, 128), 0)
    lane8 = lax.broadcasted_iota(jnp.int32, (8, 128), 1)
    i_rows = dot01(lane8 == sub8, g, tdims)
    b_rows = dot01(lane8 == sub8 + M_HEADS, bcum, tdims)

    def conv_silu(col0):
        acc = bc_ref[:, col0:col0 + M_DK] + xbuf_ref[8:8 + L, col0:col0 + M_DK] * wc_ref[3:4, col0:col0 + M_DK]
        for j in range(CONV_W - 1):
            acc = acc + xbuf_ref[5 + j:5 + j + L, col0:col0 + M_DK] * wc_ref[j:j + 1, col0:col0 + M_DK]
        return acc * _sigmoid(acc)

    for h in range(M_HEADS):
        q = conv_silu(h * M_DK)
        k = conv_silu(M_QK + h * M_DK) * (M_DK ** -0.5)
        v = za_ref[:, 2 * M_QK + h * M_DV:2 * M_QK + (h + 1) * M_DV]
        o = za_ref[:, 2 * M_QK + M_V + h * M_DV:2 * M_QK + M_V + (h + 1) * M_DV].astype(F32)
        i_row = i_rows[h:h + 1, :]
        b_row = b_rows[h:h + 1, :]
        i_col = g[:, h:h + 1]
        b_col = bcum[:, M_HEADS + h:M_HEADS + h + 1]
        m_prev = ms_ref[h][:, 0:1]
        dmat = jnp.where(causal, b_col - b_row + i_row, -jnp.inf)
        inter = b_col + m_prev
        m_t = jnp.maximum(inter, jnp.max(dmat, axis=-1, keepdims=True))
        w_intra = jnp.exp(dmat - m_t)
        w_inter = jnp.exp(inter - m_t)
        qb = q.astype(BF16)
        kb = k.astype(BF16)
        s = lax.dot_general(qb, kb, tdims, preferred_element_type=F32) * w_intra
        c_old = cs_ref[h]
        n_old = ns_ref[h]
        num = (jnp.dot(s.astype(BF16), v, preferred_element_type=F32)
               + w_inter * jnp.dot(qb, c_old.astype(BF16), preferred_element_type=F32))
        qn = jnp.sum(s, axis=-1, keepdims=True) + w_inter * jnp.sum(q * n_old, axis=-1, keepdims=True)
        den = jnp.maximum(jnp.abs(qn), jnp.exp(-m_t))
        hh = num / den
        m_new = m_t[L - 1:L, :]
        b_last = bcum[L - 1:L, M_HEADS + h:M_HEADS + h + 1]
        w_prev = jnp.exp(b_last + m_prev - m_new)
        kw = k * jnp.exp(b_last - b_col + i_col - m_new)
        cs_ref[h] = w_prev * c_old + lax.dot_general(kw.astype(BF16), v, (((0,), (0,)), ((), ())),
                                                     preferred_element_type=F32)
        ns_ref[h] = w_prev * n_old + jnp.sum(kw, axis=0, keepdims=True)
        ms_ref[h] = jnp.broadcast_to(m_new, (1, 128))
        hn = hh * lax.rsqrt(jnp.mean(hh * hh, axis=-1, keepdims=True) + EPS) * gmh_ref[:, h * M_DV:(h + 1) * M_DV]
        hg_ref[:, h * M_DV:(h + 1) * M_DV] = (hn * _sigmoid(o)).astype(BF16)

    xbuf_ref[0:8, :] = xbuf_ref[L:L + 8, :]

    @pl.when(c == nc - 1)
    def _():
        cout_ref[...] = cs_ref[...]
        nout_ref[...] = ns_ref[...]
        mout_ref[...] = ms_ref[...]


def _mlstm(za, zi, seq, w_conv, b_conv, bif, g_mh, c0, n0, m0, cv0):
    b = za.shape[0] // seq
    L = min(seq, 256)
    nc = seq // L
    za3 = za.reshape(b, seq, W_A)
    zi3 = zi.reshape(b, seq, W_GATE)
    shared = c0.shape[0] == 1
    bsel = (lambda bi: 0) if shared else (lambda bi: bi)
    kern = functools.partial(_mlstm_kernel, L=L, nc=nc)
    st4 = lambda last2: pl.BlockSpec((None, M_HEADS) + last2, lambda bi, ci: (bsel(bi), 0, 0, 0))
    out4 = lambda last2: pl.BlockSpec((None, M_HEADS) + last2, lambda bi, ci: (bi, 0, 0, 0))
    return pl.pallas_call(
        kern,
        out_shape=(jax.ShapeDtypeStruct((b, seq, M_V), BF16),
                   jax.ShapeDtypeStruct((b, M_HEADS, M_DK, M_DV), F32),
                   jax.ShapeDtypeStruct((b, M_HEADS, 1, M_DK), F32),
                   jax.ShapeDtypeStruct((b, M_HEADS, 1, 128), F32)),
        grid=(b, nc),
        in_specs=[pl.BlockSpec((None, L, W_A), lambda bi, ci: (bi, ci, 0)),
                  pl.BlockSpec((None, L, W_GATE), lambda bi, ci: (bi, ci, 0)),
                  _resident((CONV_W, 2 * M_QK)), _resident((1, 2 * M_QK)), _resident((1, W_GATE)),
                  _resident((1, M_V)),
                  st4((M_DK, M_DV)), st4((1, M_DK)), st4((1, 128)),
                  pl.BlockSpec((None, 8, 2 * M_QK), lambda bi, ci: (bsel(bi), 0, 0))],
        out_specs=(pl.BlockSpec((None, L, M_V), lambda bi, ci: (bi, ci, 0)),
                   out4((M_DK, M_DV)), out4((1, M_DK)), out4((1, 128))),
        scratch_shapes=[pltpu.VMEM((M_HEADS, M_DK, M_DV), F32), pltpu.VMEM((M_HEADS, 1, M_DK), F32),
                        pltpu.VMEM((M_HEADS, 1, 128), F32), pltpu.VMEM((L + 8, 2 * M_QK), F32)],
        compiler_params=_cparams(("parallel", "arbitrary")),
        name="mlstm",
    )(za3, zi3, w_conv, b_conv, bif, g_mh, c0, n0, m0, cv0)


def _attn_kernel(cq_ref, ct_ref, st_ref, wn_ref, wr_ref, wrs_ref, wuk_ref, wuv_ref,
                 plat_ref, pkr_ref, klat_ref, kkr_ref, o_ref,
                 ql_ref, qr_ref, s0_ref, s_ref, mrun_ref, mb_ref, lrun_ref, acc_ref,
                 *, tq, tk, n_prefix, n_kt, causal):
    i = pl.program_id(1)
    r = A_HEADS * tq
    tdims = (((1,), (1,)), ((), ()))
    lane_chunks = tk // 128

    cq = cq_ref[...]
    qn_all = jnp.dot(cq, wn_ref[...], preferred_element_type=F32)
    qr_all = jnp.dot(cq, wr_ref[...], preferred_element_type=F32)
    qrs_all = jnp.dot(cq, wrs_ref[...], preferred_element_type=F32)
    ct = ct_ref[...]
    st = st_ref[...]
    for h in range(A_HEADS):
        qn = qn_all[:, h * NOPE_DIM:(h + 1) * NOPE_DIM].astype(BF16)
        ql = jnp.dot(qn, wuk_ref[h], preferred_element_type=F32) * ATTN_SCALE
        ql_ref[h * tq:(h + 1) * tq, :] = ql.astype(BF16)
        qr = (qr_all[:, h * 128:(h + 1) * 128] * ct + qrs_all[:, h * 128:(h + 1) * 128] * st) * ATTN_SCALE
        qr_ref[h * tq:(h + 1) * tq, :] = qr.astype(BF16)

    def scores(kl, kk):
        return (lax.dot_general(ql_ref[...], kl, tdims, preferred_element_type=F32)
                + lax.dot_general(qr_ref[:, 0:ROPE_DIM], kk, tdims, preferred_element_type=F32))

    pl_b = plat_ref[...].astype(BF16)
    s0 = scores(pl_b, pkr_ref[...].astype(BF16))
    pcol = lax.broadcasted_iota(jnp.int32, s0.shape, 1)
    s0 = jnp.where(pcol < n_prefix, s0, -jnp.inf)
    s0_ref[...] = s0
    mrun_ref[...] = s0

    def score_block(j, masked):
        start = pl.multiple_of(j * tk, tk)
        kl = klat_ref[pl.ds(start, tk), :].astype(BF16)
        kk = kkr_ref[pl.ds(start, tk), :].astype(BF16)
        s = scores(kl, kk)
        if masked:
            qpos = i * tq + (lax.broadcasted_iota(jnp.int32, s.shape, 0) & (tq - 1))
            kpos = start + lax.broadcasted_iota(jnp.int32, s.shape, 1)
            shift = CHUNK.bit_length() - 1
            s = jnp.where(jnp.right_shift(kpos, shift) <= jnp.right_shift(qpos, shift), s, -jnp.inf)
        s_ref[j] = s
        mr = mrun_ref[...]
        for c in range(lane_chunks):
            mr = jnp.maximum(mr, s[:, c * 128:(c + 1) * 128])
        mrun_ref[...] = mr

    def score_body(j, carry):
        score_block(j, False)
        return carry

    if causal:
        last = (i * tq) // tk
        lax.fori_loop(0, last, score_body, 0)
        score_block(last, True)
        n_blocks = last + 1
    else:
        lax.fori_loop(0, n_kt, score_body, 0)
        n_blocks = n_kt

    mb = jnp.broadcast_to(jnp.max(mrun_ref[...], axis=-1, keepdims=True), (r, 128))
    mb_ref[...] = mb
    p0 = jnp.exp(s0_ref[...] - mb)
    lrun_ref[...] = p0
    acc_ref[...] = jnp.dot(p0.astype(BF16), pl_b, preferred_element_type=F32)

    def value_body(j, carry):
        start = pl.multiple_of(j * tk, tk)
        kl = klat_ref[pl.ds(start, tk), :].astype(BF16)
        mbw = jnp.concatenate([mb_ref[...]] * lane_chunks, axis=1)
        p = jnp.exp(s_ref[j] - mbw)
        lr = lrun_ref[...]
        for c in range(lane_chunks):
            lr = lr + p[:, c * 128:(c + 1) * 128]
        lrun_ref[...] = lr
        acc_ref[...] += jnp.dot(p.astype(BF16), kl, preferred_element_type=F32)
        return carry

    lax.fori_loop(0, n_blocks, value_body, 0)

    o = acc_ref[...] / jnp.sum(lrun_ref[...], axis=-1, keepdims=True)
    for h in range(A_HEADS):
        oh = o[h * tq:(h + 1) * tq, :].astype(BF16)
        o_ref[:, h * V_HEAD:(h + 1) * V_HEAD] = jnp.dot(oh, wuv_ref[h], preferred_element_type=F32).astype(BF16)


def _attention(cqn, seq, ctab, stab, wn, wr, wrs, wuk, wuv, plat, pkr, n_prefix, klat, kkr, causal):
    b = cqn.shape[0] // seq
    tq = min(seq, 128)
    nq = seq // tq
    tkeys = klat.shape[1]
    tk = 256
    n_kt = tkeys // tk
    r = A_HEADS * tq
    cq3 = cqn.reshape(b, seq, Q_LORA)
    psel = (lambda bi: 0) if plat.shape[0] == 1 else (lambda bi: bi)
    kern = functools.partial(_attn_kernel, tq=tq, tk=tk, n_prefix=n_prefix, n_kt=n_kt, causal=causal)
    tab = pl.BlockSpec((tq, 128), lambda bi, qi: (qi, 0))
    out = pl.pallas_call(
        kern,
        out_shape=jax.ShapeDtypeStruct((b, seq, A_HEADS * V_HEAD), BF16),
        grid=(b, nq),
        in_specs=[pl.BlockSpec((None, tq, Q_LORA), lambda bi, qi: (bi, qi, 0)), tab, tab,
                  _resident((Q_LORA, A_HEADS * NOPE_DIM)), _resident((Q_LORA, A_HEADS * 128)),
                  _resident((Q_LORA, A_HEADS * 128)), _resident((A_HEADS, NOPE_DIM, KV_LORA)),
                  _resident((A_HEADS, KV_LORA, V_HEAD)),
                  pl.BlockSpec((None, PREFIX_PAD, KV_LORA), lambda bi, qi: (psel(bi), 0, 0)),
                  pl.BlockSpec((None, PREFIX_PAD, ROPE_DIM), lambda bi, qi: (psel(bi), 0, 0)),
                  pl.BlockSpec((None, tkeys, KV_LORA), lambda bi, qi: (bi, 0, 0)),
                  pl.BlockSpec((None, tkeys, ROPE_DIM), lambda bi, qi: (bi, 0, 0))],
        out_specs=pl.BlockSpec((None, tq, A_HEADS * V_HEAD), lambda bi, qi: (bi, qi, 0)),
        scratch_shapes=[pltpu.VMEM((r, KV_LORA), BF16), pltpu.VMEM((r, 128), BF16),
                        pltpu.VMEM((r, PREFIX_PAD), F32), pltpu.VMEM((n_kt, r, tk), F32),
                        pltpu.VMEM((r, 128), F32), pltpu.VMEM((r, 128), F32), pltpu.VMEM((r, 128), F32),
                        pltpu.VMEM((r, KV_LORA), F32)],
        compiler_params=_cparams(("parallel", "arbitrary")),
        name="attention",
    )(cq3, ctab, stab, wn, wr, wrs, wuk, wuv, plat, pkr, klat, kkr)
    return out.reshape(b * seq, A_HEADS * V_HEAD)


def _merge_kernel(hg_ref, ov_ref, gab_ref, x_ref, wa_ref, wb_ref, wo_ref, gf_ref, wrh_ref, wrl_ref, br_ref,
                  x1_ref, x2_ref, ti_ref, tg_ref, tr_ref, cnt_ref, carry_ref):
    i = pl.program_id(0)

    @pl.when(i == 0)
    def _():
        carry_ref[...] = jnp.zeros_like(carry_ref)

    tm = x_ref.shape[0]
    ya = jnp.dot(hg_ref[...], wa_ref[...], preferred_element_type=F32)
    yb = jnp.dot(ov_ref[...], wb_ref[...], preferred_element_type=F32)
    mixed = (_sigmoid(gab_ref[:, 0:D_MODEL].astype(F32)) * ya
             + _sigmoid(gab_ref[:, D_MODEL:2 * D_MODEL].astype(F32)) * yb)
    x1 = x_ref[...] + jnp.dot(mixed.astype(BF16), wo_ref[...], preferred_element_type=F32)
    x1_ref[...] = x1
    x2 = x1 * lax.rsqrt(jnp.mean(x1 * x1, axis=-1, keepdims=True) + EPS) * gf_ref[...]
    x2_ref[...] = x2

    lane = lax.broadcasted_iota(jnp.int32, (tm, 128), 1)
    lane_f = lane.astype(F32)
    x2_hi = x2.astype(BF16)
    x2_lo = (x2 - x2_hi.astype(F32)).astype(BF16)
    logits = (jnp.dot(x2_hi, wrh_ref[...], preferred_element_type=F32)
              + jnp.dot(x2_lo, wrh_ref[...], preferred_element_type=F32)
              + jnp.dot(x2_hi, wrl_ref[...], preferred_element_type=F32) + br_ref[...])
    cur = jnp.where(lane < N_EXPERTS, logits, -jnp.inf)
    vals, idxs = [], []
    for _ in range(TOP_K):
        mx = jnp.max(cur, axis=-1, keepdims=True)
        idx = jnp.min(jnp.where(cur == mx, lane_f, 128.0), axis=-1, keepdims=True)
        vals.append(mx)
        idxs.append(idx)
        cur = jnp.where(lane_f == idx, -jnp.inf, cur)
    es = [jnp.exp(v - vals[0]) for v in vals]
    tot = es[0] + es[1] + es[2] + es[3]

    onehots = [(lane_f == idx) for idx in idxs]
    cnt = jnp.zeros((tm, 128), F32)
    for oh in onehots:
        cnt = cnt + jnp.where(oh, 1.0, 0.0)
    rr = lax.broadcasted_iota(jnp.int32, (tm, tm), 0)
    cc = lax.broadcasted_iota(jnp.int32, (tm, tm), 1)
    before = jnp.dot((rr > cc).astype(BF16), cnt.astype(BF16), preferred_element_type=F32) + carry_ref[0:1, :]
    ti = jnp.zeros((tm, 128), F32)
    tg = jnp.zeros((tm, 128), F32)
    tr = jnp.zeros((tm, 128), F32)
    for k in range(TOP_K):
        rank = jnp.sum(jnp.where(onehots[k], before, 0.0), axis=-1, keepdims=True)
        ti = jnp.where(lane == k, idxs[k], ti)
        tg = jnp.where(lane == k, es[k] / tot, tg)
        tr = jnp.where(lane == k, rank, tr)
    ti_ref[...] = ti.astype(jnp.int32)
    tg_ref[...] = tg
    tr_ref[...] = tr.astype(jnp.int32)
    new_carry = carry_ref[0:1, :] + jnp.sum(cnt, axis=0, keepdims=True)
    carry_ref[...] = jnp.broadcast_to(new_carry, carry_ref.shape)
    cnt_ref[...] = jnp.broadcast_to(new_carry, cnt_ref.shape).astype(jnp.int32)


def _merge_route(hg, ov, gab, x2d, wa, wb, wo, g_ffn, w_router_hi, w_router_lo, b_router):
    m = x2d.shape[0]
    tm = min(512, m)
    row = lambda w: pl.BlockSpec((tm, w), lambda i: (i, 0))
    sq = _resident((D_MODEL, D_MODEL))
    return pl.pallas_call(
        _merge_kernel,
        out_shape=(jax.ShapeDtypeStruct((m, D_MODEL), F32), jax.ShapeDtypeStruct((m, D_MODEL), F32),
                   jax.ShapeDtypeStruct((m, 128), jnp.int32), jax.ShapeDtypeStruct((m, 128), F32),
                   jax.ShapeDtypeStruct((m, 128), jnp.int32), jax.ShapeDtypeStruct((8, 128), jnp.int32)),
        grid=(m // tm,),
        in_specs=[row(M_V), row(A_HEADS * V_HEAD), row(W_G), row(D_MODEL), sq, sq, sq,
                  _resident((1, D_MODEL)), _resident((D_MODEL, 128)), _resident((D_MODEL, 128)),
                  _resident((1, 128))],
        out_specs=(row(D_MODEL), row(D_MODEL), row(128), row(128), row(128),
                   pl.BlockSpec((8, 128), lambda i: (0, 0))),
        scratch_shapes=[pltpu.VMEM((8, 128), F32)],
        compiler_params=_cparams(("arbitrary",)),
        name="merge_route",
    )(hg, ov, gab, x2d, wa, wb, wo, g_ffn, w_router_hi, w_router_lo, b_router)


def _row_copy(src_ref, src_row, dst_ref, dst_row, sem):
    return pltpu.make_async_copy(src_ref.at[pl.ds(src_row, 1), :], dst_ref.at[pl.ds(dst_row, 1), :], sem)


def _dispatch_kernel(zt_ref, dest_ref, x_ref, xs_ref, zbuf_ref, sem, zsem):
    i = pl.program_id(0)
    tr = x_ref.shape[0]
    tm = zbuf_ref.shape[0]

    @pl.when(i == 0)
    def _():
        zbuf_ref[...] = jnp.zeros_like(zbuf_ref)
        for e in range(2 * N_EXPERTS):
            @pl.when(zt_ref[e] >= 0)
            def _():
                start = pl.multiple_of(zt_ref[e], tm)
                pltpu.make_async_copy(zbuf_ref, xs_ref.at[pl.ds(start, tm), :], zsem).start()
        for e in range(2 * N_EXPERTS):
            @pl.when(zt_ref[e] >= 0)
            def _():
                pltpu.make_async_copy(zbuf_ref, xs_ref.at[pl.ds(0, tm), :], zsem).wait()

    def issue(r, carry):
        for k in range(TOP_K):
            _row_copy(x_ref, r, xs_ref, dest_ref[0, 0, r * TOP_K + k], sem).start(priority=k % 2)
        return carry

    lax.fori_loop(0, tr, issue, 0, unroll=8)
    for _ in range(TOP_K):
        pltpu.make_async_copy(x_ref, xs_ref.at[pl.ds(0, tr), :], sem).wait()


def _dispatch(x2, dest, zero_tiles, n_rows):
    m, w = x2.shape
    tr = min(256, m)
    nt = m // tr
    dest3 = dest.reshape(nt, 1, tr * TOP_K)
    grid_spec = pltpu.PrefetchScalarGridSpec(
        num_scalar_prefetch=1,
        grid=(nt,),
        in_specs=[pl.BlockSpec((1, 1, tr * TOP_K), lambda i, zt: (i, 0, 0), memory_space=pltpu.SMEM),
                  pl.BlockSpec((tr, w), lambda i, zt: (i, 0))],
        out_specs=pl.BlockSpec(memory_space=pl.ANY),
        scratch_shapes=[pltpu.VMEM((MOE_TILE, w), x2.dtype), pltpu.SemaphoreType.DMA(()),
                        pltpu.SemaphoreType.DMA(())],
    )
    return pl.pallas_call(
        _dispatch_kernel,
        out_shape=jax.ShapeDtypeStruct((n_rows, w), x2.dtype),
        grid_spec=grid_spec,
        compiler_params=_cparams(("arbitrary",)),
        name="moe_dispatch",
    )(zero_tiles, dest3, x2)


def _expert_kernel(te_ref, nu_ref, xs_ref, wgu_ref, bgu_ref, wdn_ref, bdn_ref, ys_ref, wgu_b, wdn_b):
    i = pl.program_id(0)
    prev = te_ref[jnp.maximum(i - 1, 0)]

    @pl.when(i < nu_ref[0])
    def _():
        @pl.when((i == 0) | (te_ref[i] != prev))
        def _():
            rc = 128
            for c in range(D_MODEL // rc):
                wgu_b[c * rc:(c + 1) * rc, :] = wgu_ref[c * rc:(c + 1) * rc, :].astype(BF16)
                wdn_b[c * rc:(c + 1) * rc, :] = wdn_ref[c * rc:(c + 1) * rc, :].astype(BF16)

        xb = xs_ref[...].astype(BF16)
        fc = 512
        y = jnp.zeros((xs_ref.shape[0], D_MODEL), F32) + bdn_ref[...]
        for f0 in range(0, D_FF, fc):
            def proj(c0):
                return jnp.dot(xb, wgu_b[:, c0:c0 + fc], preferred_element_type=F32) + bgu_ref[:, c0:c0 + fc]
            gate = jnp.minimum(proj(f0), SWIGLU_LIMIT)
            up = jnp.clip(proj(D_FF + f0), -SWIGLU_LIMIT, SWIGLU_LIMIT)
            hid = (up + 1.0) * gate * _sigmoid(SWIGLU_ALPHA * gate)
            y = y + jnp.dot(hid.astype(BF16), wdn_b[f0:f0 + fc, :], preferred_element_type=F32)
        ys_ref[...] = y

    @pl.when(i >= nu_ref[0])
    def _():
        ys_ref[...] = jnp.zeros_like(ys_ref)


def _experts(xs, tile_expert, n_used, w_gu, b_gu, w_dn, b_dn):
    n_rows, w = xs.shape
    tm = MOE_TILE
    nt = n_rows // tm
    tile = lambda i, te, nu: (jnp.minimum(i, jnp.maximum(nu[0] - 1, 0)), 0)
    grid_spec = pltpu.PrefetchScalarGridSpec(
        num_scalar_prefetch=2,
        grid=(nt,),
        in_specs=[pl.BlockSpec((tm, w), tile),
                  pl.BlockSpec((None, D_MODEL, 2 * D_FF), lambda i, te, nu: (te[i], 0, 0)),
                  pl.BlockSpec((None, 1, 2 * D_FF), lambda i, te, nu: (te[i], 0, 0)),
                  pl.BlockSpec((None, D_FF, D_MODEL), lambda i, te, nu: (te[i], 0, 0)),
                  pl.BlockSpec((None, 1, D_MODEL), lambda i, te, nu: (te[i], 0, 0))],
        out_specs=pl.BlockSpec((tm, w), lambda i, te, nu: (i, 0)),
        scratch_shapes=[pltpu.VMEM((D_MODEL, 2 * D_FF), BF16), pltpu.VMEM((D_FF, D_MODEL), BF16)],
    )
    return pl.pallas_call(
        _expert_kernel,
        out_shape=jax.ShapeDtypeStruct((n_rows, w), F32),
        grid_spec=grid_spec,
        compiler_params=_cparams(("arbitrary",)),
        name="moe_experts",
    )(tile_expert, n_used, xs, w_gu, b_gu, w_dn, b_dn)


def _combine_kernel(dcur_ref, dnext_ref, x1_ref, tg_ref, gfin_ref, ys_ref, y_ref, buf_ref, sem_ref, *, nt):
    i = pl.program_id(0)
    tc = x1_ref.shape[0]

    def issue(dref, slot):
        def body(r, carry):
            for k in range(TOP_K):
                pltpu.make_async_copy(ys_ref.at[pl.ds(dref[0, 0, r * TOP_K + k], 1), :],
                                      buf_ref.at[slot, k, pl.ds(r, 1), :], sem_ref.at[slot]).start(priority=k % 2)
            return carry
        lax.fori_loop(0, tc, body, 0, unroll=8)

    slot = i % 2

    @pl.when(i == 0)
    def _():
        issue(dcur_ref, 0)

    @pl.when(i + 1 < nt)
    def _():
        issue(dnext_ref, 1 - slot)

    for k in range(TOP_K):
        pltpu.make_async_copy(ys_ref.at[pl.ds(0, tc), :], buf_ref.at[slot, k], sem_ref.at[slot]).wait()

    x = x1_ref[...]
    for k in range(TOP_K):
        x = x + tg_ref[:, k:k + 1] * buf_ref[slot, k]
    y_ref[...] = x * lax.rsqrt(jnp.mean(x * x, axis=-1, keepdims=True) + EPS) * gfin_ref[...]


def _combine(dest, x1, tg, g_final, ys):
    m = x1.shape[0]
    tc = min(256, m)
    nt = m // tc
    dest3 = dest.reshape(nt, 1, tc * TOP_K)
    kern = functools.partial(_combine_kernel, nt=nt)
    smem = lambda f: pl.BlockSpec((1, 1, tc * TOP_K), f, memory_space=pltpu.SMEM)
    return pl.pallas_call(
        kern,
        out_shape=jax.ShapeDtypeStruct((m, D_MODEL), F32),
        grid=(nt,),
        in_specs=[smem(lambda i: (i, 0, 0)), smem(lambda i: (jnp.minimum(i + 1, nt - 1), 0, 0)),
                  pl.BlockSpec((tc, D_MODEL), lambda i: (i, 0)), pl.BlockSpec((tc, 128), lambda i: (i, 0)),
                  _resident((1, D_MODEL)), pl.BlockSpec(memory_space=pl.ANY)],
        out_specs=pl.BlockSpec((tc, D_MODEL), lambda i: (i, 0)),
        scratch_shapes=[pltpu.VMEM((2, TOP_K, tc, D_MODEL), F32), pltpu.SemaphoreType.DMA((2,))],
        compiler_params=_cparams(("arbitrary",)),
        name="moe_combine",
    )(dest3, dest3, x1, tg, g_final, ys)


def _rope_tables(pos):
    half = ROPE_DIM // 2
    freqs = ROPE_THETA ** (-jnp.arange(half, dtype=F32) / half)
    ang = pos.astype(F32)[:, None] * freqs[None, :]
    cos, sin = jnp.cos(ang), jnp.sin(ang)
    return jnp.concatenate([cos, cos], axis=-1), jnp.concatenate([-sin, sin], axis=-1)


def _tile_rows(t, rows):
    return t if t.shape[0] >= rows else jnp.tile(t, (rows // t.shape[0], 1))


def _pad_lanes(t, width):
    return jnp.pad(t, ((0, 0), (0, width - t.shape[1])))


def _moe(x1, x2n, ti, tg, tr, cnt, g_final, w_gu, b_gu, w_dn, b_dn):
    m = x1.shape[0]
    a = m * TOP_K
    tm = MOE_TILE
    n_tiles = -(-a // tm) + N_EXPERTS
    counts = cnt[0, :N_EXPERTS]
    pcounts = (counts + tm - 1) // tm * tm
    pend = jnp.cumsum(pcounts)
    pstart = pend - pcounts
    top_i = ti[:, :TOP_K]
    eids = jnp.arange(N_EXPERTS, dtype=jnp.int32)
    dest = tr[:, :TOP_K] + jnp.sum(jnp.where(top_i[..., None] == eids, pstart, 0), axis=-1)
    n_used = (pend[-1] // tm).astype(jnp.int32)
    tiles = jnp.minimum(jnp.arange(n_tiles, dtype=jnp.int32), n_used - 1) * tm
    tile_expert = jnp.minimum(jnp.sum(pend[None, :] <= tiles[:, None], axis=-1), N_EXPERTS - 1).astype(jnp.int32)
    tail = n_used + jnp.arange(N_EXPERTS, dtype=jnp.int32)
    zero_tiles = jnp.concatenate([jnp.where(pcounts > 0, pend - tm, -1),
                                  jnp.where(tail < n_tiles, tail * tm, -1)]).astype(jnp.int32)
    xs = _dispatch(x2n, dest.astype(jnp.int32), zero_tiles, n_tiles * tm)
    ys = _experts(xs, tile_expert, n_used.reshape(1), w_gu, b_gu, w_dn, b_dn)
    return _combine(dest.astype(jnp.int32), x1, tg, g_final, ys)


def kernel(x_prompt, x_sample, cache_kv_latent, cache_k_rope, state_mlstm_C, state_mlstm_n, state_mlstm_m, state_conv, meta_tokens, g_mix_norm, w_in, b_if, w_conv, b_conv, g_mh_norm, w_proj_a, g_q_norm, g_kv_norm, w_uq, w_uk, w_uv, w_proj_b, w_out, g_ffn_norm, w_router, b_router, w_gate_up, b_gate_up, w_down, b_down, g_final_norm):
    bsz, seq = x_prompt.shape[0], x_prompt.shape[1]
    dbs, dseq = x_sample.shape[0], x_sample.shape[1]
    past = cache_kv_latent.shape[2]
    assert w_in.shape[0] == 1, "single-layer trunk"
    l = 0

    wi = w_in[l]
    o_gate = 2 * M_QK + 2 * M_V
    o_cq = o_gate + 2 * M_HEADS
    o_ckv = o_cq + Q_LORA
    o_kr = o_ckv + KV_LORA
    o_g = o_kr + ROPE_DIM
    swap = np.concatenate([np.arange(ROPE_DIM // 2, ROPE_DIM), np.arange(ROPE_DIM // 2)])
    w_kr = wi[:, o_kr:o_g]
    w_cat = jnp.concatenate([wi[:, :o_gate], _pad_lanes(wi[:, o_gate:o_cq], W_GATE), wi[:, o_cq:o_kr], w_kr,
                             w_kr[:, swap], wi[:, o_g:]], axis=1).astype(BF16)
    bif = _pad_lanes(b_if[l][None, :], W_GATE)
    uq = w_uq[l].reshape(Q_LORA, A_HEADS, NOPE_DIM + ROPE_DIM)
    wn = uq[:, :, :NOPE_DIM].reshape(Q_LORA, A_HEADS * NOPE_DIM).astype(BF16)
    uq_r = uq[:, :, NOPE_DIM:]
    pad_r = lambda t: jnp.pad(t, ((0, 0), (0, 0), (0, 128 - ROPE_DIM))).reshape(Q_LORA, A_HEADS * 128).astype(BF16)
    wr, wrs = pad_r(uq_r), pad_r(uq_r[:, :, swap])
    wuk = jnp.transpose(w_uk[l], (1, 2, 0)).astype(BF16)
    wuv = jnp.transpose(w_uv[l], (1, 0, 2)).astype(BF16)
    wa, wb, wo = w_proj_a[l].astype(BF16), w_proj_b[l].astype(BF16), w_out[l].astype(BF16)
    w_rt = _pad_lanes(w_router[l], 128)
    w_rt_hi = w_rt.astype(BF16)
    w_rt_lo = (w_rt - w_rt_hi.astype(F32)).astype(BF16)
    b_rt = _pad_lanes(b_router[l][None, :], 128)
    g_mix, g_q, g_kv = g_mix_norm[l][None, :], g_q_norm[l][None, :], g_kv_norm[l][None, :]
    g_mh, g_ffn, g_fin = g_mh_norm[l][None, :], g_ffn_norm[l][None, :], g_final_norm[None, :]
    wcv, bcv = w_conv[l], b_conv[l][None, :]
    w_gu, b_gu = w_gate_up[l], b_gate_up[l][:, None, :]
    w_dn, b_dn = w_down[l], b_down[l][:, None, :]

    ct_m, st_m = _rope_tables(jnp.arange(N_META))
    ct_p, st_p = _rope_tables(N_META + jnp.arange(seq))
    ct_s, st_s = _rope_tables(N_META + past + jnp.arange(dseq))

    def stream_tables(ct, st, m_rows):
        rows = max(ct.shape[0], min(512, m_rows))
        return _tile_rows(ct, rows), _tile_rows(st, rows)

    def prefix(lat, kr):
        pad = ((0, 0), (0, PREFIX_PAD - lat.shape[1]), (0, 0))
        return jnp.pad(lat, pad), jnp.pad(kr, pad)

    za_m, zi_m, _, lat_m, kr_m, _, tail_m = _inproj(meta_tokens, N_META, g_mix, w_cat, g_q, g_kv, ct_m, st_m)
    npad = CHUNK - N_META
    za_mp = jnp.concatenate([jnp.zeros((npad, W_A), BF16), za_m], axis=0)
    lane = jnp.arange(W_GATE)
    neutral = jnp.where(lane < M_HEADS, -1e30, jnp.where(lane < 2 * M_HEADS, 1e30, 0.0)).astype(F32)
    zi_mp = jnp.concatenate([jnp.broadcast_to(neutral, (npad, W_GATE)), zi_m], axis=0)
    zeros_state = (jnp.zeros((1, M_HEADS, M_DK, M_DV), F32), jnp.zeros((1, M_HEADS, 1, M_DK), F32),
                   jnp.zeros((1, M_HEADS, 1, 128), F32), jnp.zeros((1, 8, 2 * M_QK), F32))
    _, c_m, n_m, m_m = _mlstm(za_mp, zi_mp, CHUNK, wcv, bcv, bif, g_mh, *zeros_state)

    def layer_tail(x2d, hg, ov, gab):
        x1, x2n, ti, tg, tr, cnt = _merge_route(hg.reshape(x2d.shape[0], M_V), ov, gab, x2d, wa, wb, wo, g_ffn,
                                                w_rt_hi, w_rt_lo, b_rt)
        return _moe(x1, x2n, ti, tg, tr, cnt, g_fin, w_gu, b_gu, w_dn, b_dn)

    xp = x_prompt.reshape(bsz * seq, D_MODEL)
    ctp, stp = stream_tables(ct_p, st_p, bsz * seq)
    za, zi, cqn, lat, kr, gab, tail = _inproj(xp, seq, g_mix, w_cat, g_q, g_kv, ctp, stp)
    hg, c_p, n_p, m_p = _mlstm(za, zi, seq, wcv, bcv, bif, g_mh, c_m, n_m, m_m, tail_m)
    plat, pkr = prefix(lat_m[None], kr_m[None])
    ctq, stq = _pad_lanes(ct_p, 128), _pad_lanes(st_p, 128)
    ov = _attention(cqn, seq, ctq, stq, wn, wr, wrs, wuk, wuv, plat, pkr, N_META,
                    lat.reshape(bsz, seq, KV_LORA), kr.reshape(bsz, seq, ROPE_DIM), True)
    y_prompt = layer_tail(xp, hg, ov, gab).reshape(bsz, seq, D_MODEL)

    xs2 = x_sample.reshape(dbs * dseq, D_MODEL)
    cts, sts = stream_tables(ct_s, st_s, dbs * dseq)
    za, zi, cqn, lat_s, kr_s, gab, tail_s = _inproj(xs2, dseq, g_mix, w_cat, g_q, g_kv, cts, sts)
    cv0 = jnp.pad(state_conv[l], ((0, 0), (8 - (CONV_W - 1), 0), (0, 0)))
    m0 = jnp.broadcast_to(state_mlstm_m[l][:, :, None, None], (dbs, M_HEADS, 1, 128))
    hg, c_s, n_s, m_s = _mlstm(za, zi, dseq, wcv, bcv, bif, g_mh, state_mlstm_C[l],
                               state_mlstm_n[l][:, :, None, :], m0, cv0)
    plat, pkr = prefix(lat_s.reshape(dbs, dseq, KV_LORA), kr_s.reshape(dbs, dseq, ROPE_DIM))
    ov = _attention(cqn, dseq, _pad_lanes(ct_s, 128), _pad_lanes(st_s, 128), wn, wr, wrs, wuk, wuv, plat, pkr,
                    dseq, cache_kv_latent[l], cache_k_rope[l], False)
    y_sample = layer_tail(xs2, hg, ov, gab).reshape(dbs, dseq, D_MODEL)

    def with_meta(meta_rows, frames, width):
        return jnp.concatenate([jnp.broadcast_to(meta_rows[None], (bsz, N_META, width)),
                                frames.reshape(bsz, seq, width)], axis=1)[None]

    p_lat = with_meta(lat_m, lat, KV_LORA)
    p_kr = with_meta(kr_m, kr, ROPE_DIM)
    tail3 = lambda t: t[:, 8 - (CONV_W - 1):, :][None]
    return (y_prompt, y_sample, p_lat, p_kr, c_p[None], n_p[:, :, 0, :][None], m_p[:, :, 0, 0][None], tail3(tail),
            lat_s.reshape(dbs, dseq, KV_LORA)[None], kr_s.reshape(dbs, dseq, ROPE_DIM)[None], c_s[None],
            n_s[:, :, 0, :][None], m_s[:, :, 0, 0][None], tail3(tail_s))
```

```python
import functools

import jax
import jax.numpy as jnp
import numpy as np
from jax import lax
from jax.experimental import pallas as pl
from jax.experimental.pallas import tpu as pltpu

F32 = jnp.float32
BF16 = jnp.bfloat16
HIGHEST = lax.Precision.HIGHEST

D_MODEL = 1024
N_META = 16
CHUNK = 64
M_HEADS = 4
M_DK = 256
M_DV = 256
M_QK = M_HEADS * M_DK
M_V = M_HEADS * M_DV
CONV_W = 4
A_HEADS = 8
NOPE_DIM = 128
ROPE_DIM = 64
V_HEAD = 128
Q_LORA = 384
KV_LORA = 256
ROPE_THETA = 10000.0
ATTN_SCALE = (NOPE_DIM + ROPE_DIM) ** -0.5
QUERY_SCALE = ATTN_SCALE * float(np.log2(np.e))
N_EXPERTS = 32
TOP_K = 4
D_FF = 1024
SWIGLU_LIMIT = 7.0
SWIGLU_ALPHA = 1.702
EPS = 1e-6

COL_A = 0
W_A = 2 * M_QK + 2 * M_V
COL_GATE = COL_A + W_A
W_GATE = 128
COL_B = COL_GATE + W_GATE
W_B = Q_LORA + KV_LORA + 2 * ROPE_DIM
COL_G = COL_B + W_B
W_G = 2 * D_MODEL
W_IN_COLS = COL_G + W_G

V7X_VMEM_BYTES = 64 * 2**20
VMEM_LIMIT = V7X_VMEM_BYTES - 8 * 2**20

PREFIX_PAD = 128
MOE_TILE = 512


def _sigmoid(x):
    return 1.0 / (1.0 + jnp.exp(-x))


def _cparams(sem):
    return pltpu.CompilerParams(dimension_semantics=sem, vmem_limit_bytes=VMEM_LIMIT)


def _resident(shape):
    nd = len(shape)
    return pl.BlockSpec(shape, lambda *_: (0,) * nd, pipeline_mode=pl.Buffered(1))


def _inproj_kernel(x_ref, g_ref, w_ref, gq_ref, gkv_ref, ct_ref, st_ref,
                   za_ref, zi_ref, cq_ref, lat_ref, kr_ref, gab_ref, tail_ref, xn_ref, *, seq, spt):
    x = x_ref[...]
    xn = x * lax.rsqrt(jnp.mean(x * x, axis=-1, keepdims=True) + EPS) * g_ref[...]
    xn_ref[...] = xn.astype(BF16)
    xb = xn_ref[...]
    rows = x.shape[0]
    for c in range(W_A // 1024):
        acc = jnp.dot(xb, w_ref[:, COL_A + c * 1024:COL_A + (c + 1) * 1024], preferred_element_type=F32)
        za_ref[:, c * 1024:(c + 1) * 1024] = acc.astype(BF16)
        if c * 1024 < 2 * M_QK:
            if spt == 1:
                tail_ref[0, :, c * 1024:(c + 1) * 1024] = acc[rows - 8:rows, :]
            else:
                for s in range(spt):
                    tail_ref[s, :, c * 1024:(c + 1) * 1024] = acc[(s + 1) * seq - 8:(s + 1) * seq, :]
    zi_ref[...] = jnp.dot(xb, w_ref[:, COL_GATE:COL_GATE + W_GATE], preferred_element_type=F32)
    zb = jnp.dot(xb, w_ref[:, COL_B:COL_B + W_B], preferred_element_type=F32)
    cq = zb[:, 0:Q_LORA]
    cq_ref[...] = (cq * lax.rsqrt(jnp.mean(cq * cq, axis=-1, keepdims=True) + EPS) * gq_ref[...]).astype(BF16)
    ckv = zb[:, Q_LORA:Q_LORA + KV_LORA]
    lat_ref[...] = ckv * lax.rsqrt(jnp.mean(ckv * ckv, axis=-1, keepdims=True) + EPS) * gkv_ref[...]
    k_r = zb[:, Q_LORA + KV_LORA:Q_LORA + KV_LORA + ROPE_DIM]
    k_rs = zb[:, Q_LORA + KV_LORA + ROPE_DIM:W_B]
    kr_ref[...] = k_r * ct_ref[...] + k_rs * st_ref[...]
    for c in range(W_G // 1024):
        acc = jnp.dot(xb, w_ref[:, COL_G + c * 1024:COL_G + (c + 1) * 1024], preferred_element_type=F32)
        gab_ref[:, c * 1024:(c + 1) * 1024] = acc.astype(BF16)


def _inproj(x2d, seq, g_mix, w_cat, g_q, g_kv, ctab, stab):
    m = x2d.shape[0]
    nseq = m // seq
    tm = min(512, m)
    spt = max(1, tm // seq)
    tps = max(1, seq // tm)
    nt = m // tm
    kern = functools.partial(_inproj_kernel, seq=seq, spt=spt)
    row = lambda w: pl.BlockSpec((tm, w), lambda i: (i, 0))
    tab = pl.BlockSpec((tm, ROPE_DIM), lambda i: (i % tps, 0))
    if spt == 1:
        tail_spec = pl.BlockSpec((1, 8, 2 * M_QK), lambda i: (i // tps, 0, 0))
    else:
        tail_spec = pl.BlockSpec((spt, 8, 2 * M_QK), lambda i: (i, 0, 0))
    return pl.pallas_call(
        kern,
        out_shape=(jax.ShapeDtypeStruct((m, W_A), BF16), jax.ShapeDtypeStruct((m, W_GATE), F32),
                   jax.ShapeDtypeStruct((m, Q_LORA), BF16), jax.ShapeDtypeStruct((m, KV_LORA), F32),
                   jax.ShapeDtypeStruct((m, ROPE_DIM), F32), jax.ShapeDtypeStruct((m, W_G), BF16),
                   jax.ShapeDtypeStruct((nseq, 8, 2 * M_QK), F32)),
        grid=(nt,),
        in_specs=[row(D_MODEL), _resident((1, D_MODEL)), _resident((D_MODEL, W_IN_COLS)),
                  _resident((1, Q_LORA)), _resident((1, KV_LORA)), tab, tab],
        out_specs=(row(W_A), row(W_GATE), row(Q_LORA), row(KV_LORA), row(ROPE_DIM), row(W_G), tail_spec),
        scratch_shapes=[pltpu.VMEM((tm, D_MODEL), BF16)],
        compiler_params=_cparams(("arbitrary",)),
        name="inproj",
    )(x2d, g_mix, w_cat, g_q, g_kv, ctab, stab)


def _mlstm_kernel(za_ref, zi_ref, wc_ref, bc_ref, bif_ref, gmh_ref, c0_ref, n0_ref, m0_ref, cv0_ref,
                  hg_ref, cout_ref, nout_ref, mout_ref, cs_ref, ns_ref, ms_ref, xbuf_ref, *, L, nc):
    c = pl.program_id(1)

    @pl.when(c == 0)
    def _():
        cs_ref[...] = c0_ref[...]
        ns_ref[...] = n0_ref[...]
        ms_ref[...] = m0_ref[...]
        xbuf_ref[0:8, :] = cv0_ref[...]

    xbuf_ref[8:8 + L, :] = za_ref[:, 0:2 * M_QK].astype(F32)

    g = zi_ref[...] + bif_ref[...]
    lf = jnp.minimum(g, 0.0) - jnp.log(1.0 + jnp.exp(-jnp.abs(g)))
    row = lax.broadcasted_iota(jnp.int32, (L, L), 0)
    col = lax.broadcasted_iota(jnp.int32, (L, L), 1)
    causal = row >= col
    tdims = (((1,), (1,)), ((), ()))

    def dot01(a01, x, dims):
        a = a01.astype(BF16)
        x_hi = x.astype(BF16)
        x_lo = (x - x_hi.astype(F32)).astype(BF16)
        return (lax.dot_general(a, x_hi, dims, preferred_element_type=F32)
                + lax.dot_general(a, x_lo, dims, preferred_element_type=F32))

    bcum = dot01(causal, lf, (((1,), (0,)), ((), ())))
    sub8 = lax.broadcasted_iota(jnp.int32, (8, 128), 0)
    lane8 = lax.broadcasted_iota(jnp.int32, (8, 128), 1)
    i_rows = dot01(lane8 == sub8, g, tdims)
    b_rows = dot01(lane8 == sub8 + M_HEADS, bcum, tdims)

    def conv_silu(col0):
        acc = bc_ref[:, col0:col0 + M_DK] + xbuf_ref[8:8 + L, col0:col0 + M_DK] * wc_ref[3:4, col0:col0 + M_DK]
        for j in range(CONV_W - 1):
            acc = acc + xbuf_ref[5 + j:5 + j + L, col0:col0 + M_DK] * wc_ref[j:j + 1, col0:col0 + M_DK]
        return acc * _sigmoid(acc)

    for h in range(M_HEADS):
        q = conv_silu(h * M_DK)
        k = conv_silu(M_QK + h * M_DK) * (M_DK ** -0.5)
        v = za_ref[:, 2 * M_QK + h * M_DV:2 * M_QK + (h + 1) * M_DV]
        o = za_ref[:, 2 * M_QK + M_V + h * M_DV:2 * M_QK + M_V + (h + 1) * M_DV].astype(F32)
        i_row = i_rows[h:h + 1, :]
        b_row = b_rows[h:h + 1, :]
        i_col = g[:, h:h + 1]
        b_col = bcum[:, M_HEADS + h:M_HEADS + h + 1]
        m_prev = ms_ref[h][:, 0:1]
        dmat = jnp.where(causal, b_col - b_row + i_row, -jnp.inf)
        inter = b_col + m_prev
        m_t = jnp.maximum(inter, jnp.max(dmat, axis=-1, keepdims=True))
        w_intra = jnp.exp(dmat - m_t)
        w_inter = jnp.exp(inter - m_t)
        qb = q.astype(BF16)
        kb = k.astype(BF16)
        s = lax.dot_general(qb, kb, tdims, preferred_element_type=F32) * w_intra
        c_old = cs_ref[h]
        n_old = ns_ref[h]
        num = (jnp.dot(s.astype(BF16), v, preferred_element_type=F32)
               + w_inter * jnp.dot(qb, c_old.astype(BF16), preferred_element_type=F32))
        qn = jnp.sum(s, axis=-1, keepdims=True) + w_inter * jnp.sum(q * n_old, axis=-1, keepdims=True)
        den = jnp.maximum(jnp.abs(qn), jnp.exp(-m_t))
        hh = num / den
        m_new = m_t[L - 1:L, :]
        b_last = bcum[L - 1:L, M_HEADS + h:M_HEADS + h + 1]
        w_prev = jnp.exp(b_last + m_prev - m_new)
        kw = k * jnp.exp(b_last - b_col + i_col - m_new)
        cs_ref[h] = w_prev * c_old + lax.dot_general(kw.astype(BF16), v, (((0,), (0,)), ((), ())),
                                                     preferred_element_type=F32)
        ns_ref[h] = w_prev * n_old + jnp.sum(kw, axis=0, keepdims=True)
        ms_ref[h] = jnp.broadcast_to(m_new, (1, 128))
        hn = hh * lax.rsqrt(jnp.mean(hh * hh, axis=-1, keepdims=True) + EPS) * gmh_ref[:, h * M_DV:(h + 1) * M_DV]
        hg_ref[:, h * M_DV:(h + 1) * M_DV] = (hn * _sigmoid(o)).astype(BF16)

    xbuf_ref[0:8, :] = xbuf_ref[L:L + 8, :]

    @pl.when(c == nc - 1)
    def _():
        cout_ref[...] = cs_ref[...]
        nout_ref[...] = ns_ref[...]
        mout_ref[...] = ms_ref[...]


def _mlstm(za, zi, seq, w_conv, b_conv, bif, g_mh, c0, n0, m0, cv0):
    b = za.shape[0] // seq
    L = min(seq, 256)
    nc = seq // L
    za3 = za.reshape(b, seq, W_A)
    zi3 = zi.reshape(b, seq, W_GATE)
    shared = c0.shape[0] == 1
    bsel = (lambda bi: 0) if shared else (lambda bi: bi)
    kern = functools.partial(_mlstm_kernel, L=L, nc=nc)
    st4 = lambda last2: pl.BlockSpec((None, M_HEADS) + last2, lambda bi, ci: (bsel(bi), 0, 0, 0))
    out4 = lambda last2: pl.BlockSpec((None, M_HEADS) + last2, lambda bi, ci: (bi, 0, 0, 0))
    return pl.pallas_call(
        kern,
        out_shape=(jax.ShapeDtypeStruct((b, seq, M_V), BF16),
                   jax.ShapeDtypeStruct((b, M_HEADS, M_DK, M_DV), F32),
                   jax.ShapeDtypeStruct((b, M_HEADS, 1, M_DK), F32),
                   jax.ShapeDtypeStruct((b, M_HEADS, 1, 128), F32)),
        grid=(b, nc),
        in_specs=[pl.BlockSpec((None, L, W_A), lambda bi, ci: (bi, ci, 0)),
                  pl.BlockSpec((None, L, W_GATE), lambda bi, ci: (bi, ci, 0)),
                  _resident((CONV_W, 2 * M_QK)), _resident((1, 2 * M_QK)), _resident((1, W_GATE)),
                  _resident((1, M_V)),
                  st4((M_DK, M_DV)), st4((1, M_DK)), st4((1, 128)),
                  pl.BlockSpec((None, 8, 2 * M_QK), lambda bi, ci: (bsel(bi), 0, 0))],
        out_specs=(pl.BlockSpec((None, L, M_V), lambda bi, ci: (bi, ci, 0)),
                   out4((M_DK, M_DV)), out4((1, M_DK)), out4((1, 128))),
        scratch_shapes=[pltpu.VMEM((M_HEADS, M_DK, M_DV), F32), pltpu.VMEM((M_HEADS, 1, M_DK), F32),
                        pltpu.VMEM((M_HEADS, 1, 128), F32), pltpu.VMEM((L + 8, 2 * M_QK), F32)],
        compiler_params=_cparams(("parallel", "arbitrary")),
        name="mlstm",
    )(za3, zi3, w_conv, b_conv, bif, g_mh, c0, n0, m0, cv0)


def _attn_kernel(cq_ref, ct_ref, st_ref, wn_ref, wr_ref, wrs_ref, wuk_ref, wuv_ref,
                 plat_ref, pkr_ref, klat_ref, kkr_ref, o_ref,
                 ql_ref, qr_ref, s0_ref, s_ref, mrun_ref, mb_ref, lrun_ref, acc_ref,
                 *, tq, tk, n_prefix, n_kt, causal):
    i = pl.program_id(1)
    r = A_HEADS * tq
    tdims = (((1,), (1,)), ((), ()))
    lane_chunks = tk // 128

    cq = cq_ref[...]
    qn_all = jnp.dot(cq, wn_ref[...], preferred_element_type=F32)
    qr_all = jnp.dot(cq, wr_ref[...], preferred_element_type=F32)
    qrs_all = jnp.dot(cq, wrs_ref[...], preferred_element_type=F32)
    ct = ct_ref[...]
    st = st_ref[...]
    for h in range(A_HEADS):
        qn = qn_all[:, h * NOPE_DIM:(h + 1) * NOPE_DIM].astype(BF16)
        ql = jnp.dot(qn, wuk_ref[h], preferred_element_type=F32) * QUERY_SCALE
        ql_ref[h * tq:(h + 1) * tq, :] = ql.astype(BF16)
        qr = (qr_all[:, h * 128:(h + 1) * 128] * ct + qrs_all[:, h * 128:(h + 1) * 128] * st) * QUERY_SCALE
        qr_ref[h * tq:(h + 1) * tq, :] = qr.astype(BF16)

    def scores(kl, kk):
        return (lax.dot_general(ql_ref[...], kl, tdims, preferred_element_type=F32)
                + lax.dot_general(qr_ref[:, 0:ROPE_DIM], kk, tdims, preferred_element_type=F32))

    pl_b = plat_ref[...].astype(BF16)
    s0 = scores(pl_b, pkr_ref[...].astype(BF16))
    pcol = lax.broadcasted_iota(jnp.int32, s0.shape, 1)
    s0 = jnp.where(pcol < n_prefix, s0, -jnp.inf)
    s0_ref[...] = s0
    mrun_ref[...] = s0

    def score_block(j, masked):
        start = pl.multiple_of(j * tk, tk)
        kl = klat_ref[pl.ds(start, tk), :].astype(BF16)
        kk = kkr_ref[pl.ds(start, tk), :].astype(BF16)
        s = scores(kl, kk)
        if masked:
            qpos = i * tq + (lax.broadcasted_iota(jnp.int32, s.shape, 0) & (tq - 1))
            kpos = start + lax.broadcasted_iota(jnp.int32, s.shape, 1)
            shift = CHUNK.bit_length() - 1
            s = jnp.where(jnp.right_shift(kpos, shift) <= jnp.right_shift(qpos, shift), s, -jnp.inf)
        s_ref[j] = s
        mr = mrun_ref[...]
        for c in range(lane_chunks):
            mr = jnp.maximum(mr, s[:, c * 128:(c + 1) * 128])
        mrun_ref[...] = mr

    def score_body(j, carry):
        score_block(j, False)
        return carry

    if causal:
        last = (i * tq) // tk
        lax.fori_loop(0, last, score_body, 0)
        score_block(last, True)
        n_blocks = last + 1
    else:
        lax.fori_loop(0, n_kt, score_body, 0)
        n_blocks = n_kt

    mb = jnp.broadcast_to(jnp.max(mrun_ref[...], axis=-1, keepdims=True), (r, 128))
    mb_ref[...] = mb
    p0 = jnp.exp2(s0_ref[...] - mb)
    lrun_ref[...] = p0
    acc_ref[...] = jnp.dot(p0.astype(BF16), pl_b, preferred_element_type=F32)

    def value_body(j, carry):
        start = pl.multiple_of(j * tk, tk)
        kl = klat_ref[pl.ds(start, tk), :].astype(BF16)
        mbw = jnp.concatenate([mb_ref[...]] * lane_chunks, axis=1)
        p = jnp.exp2(s_ref[j] - mbw)
        lr = lrun_ref[...]
        for c in range(lane_chunks):
            lr = lr + p[:, c * 128:(c + 1) * 128]
        lrun_ref[...] = lr
        acc_ref[...] += jnp.dot(p.astype(BF16), kl, preferred_element_type=F32)
        return carry

    lax.fori_loop(0, n_blocks, value_body, 0)

    o = acc_ref[...] / jnp.sum(lrun_ref[...], axis=-1, keepdims=True)
    for h in range(A_HEADS):
        oh = o[h * tq:(h + 1) * tq, :].astype(BF16)
        o_ref[:, h * V_HEAD:(h + 1) * V_HEAD] = jnp.dot(oh, wuv_ref[h], preferred_element_type=F32).astype(BF16)


def _attention(cqn, seq, ctab, stab, wn, wr, wrs, wuk, wuv, plat, pkr, n_prefix, klat, kkr, causal):
    b = cqn.shape[0] // seq
    tq = min(seq, 128)
    nq = seq // tq
    tkeys = klat.shape[1]
    tk = 512
    n_kt = tkeys // tk
    r = A_HEADS * tq
    cq3 = cqn.reshape(b, seq, Q_LORA)
    psel = (lambda bi: 0) if plat.shape[0] == 1 else (lambda bi: bi)
    kern = functools.partial(_attn_kernel, tq=tq, tk=tk, n_prefix=n_prefix, n_kt=n_kt, causal=causal)
    tab = pl.BlockSpec((tq, 128), lambda bi, qi: (qi, 0))
    out = pl.pallas_call(
        kern,
        out_shape=jax.ShapeDtypeStruct((b, seq, A_HEADS * V_HEAD), BF16),
        grid=(b, nq),
        in_specs=[pl.BlockSpec((None, tq, Q_LORA), lambda bi, qi: (bi, qi, 0)), tab, tab,
                  _resident((Q_LORA, A_HEADS * NOPE_DIM)), _resident((Q_LORA, A_HEADS * 128)),
                  _resident((Q_LORA, A_HEADS * 128)), _resident((A_HEADS, NOPE_DIM, KV_LORA)),
                  _resident((A_HEADS, KV_LORA, V_HEAD)),
                  pl.BlockSpec((None, PREFIX_PAD, KV_LORA), lambda bi, qi: (psel(bi), 0, 0)),
                  pl.BlockSpec((None, PREFIX_PAD, ROPE_DIM), lambda bi, qi: (psel(bi), 0, 0)),
                  pl.BlockSpec((None, tkeys, KV_LORA), lambda bi, qi: (bi, 0, 0)),
                  pl.BlockSpec((None, tkeys, ROPE_DIM), lambda bi, qi: (bi, 0, 0))],
        out_specs=pl.BlockSpec((None, tq, A_HEADS * V_HEAD), lambda bi, qi: (bi, qi, 0)),
        scratch_shapes=[pltpu.VMEM((r, KV_LORA), BF16), pltpu.VMEM((r, 128), BF16),
                        pltpu.VMEM((r, PREFIX_PAD), F32), pltpu.VMEM((n_kt, r, tk), F32),
                        pltpu.VMEM((r, 128), F32), pltpu.VMEM((r, 128), F32), pltpu.VMEM((r, 128), F32),
                        pltpu.VMEM((r, KV_LORA), F32)],
        compiler_params=_cparams(("parallel", "arbitrary")),
        name="attention",
    )(cq3, ctab, stab, wn, wr, wrs, wuk, wuv, plat, pkr, klat, kkr)
    return out.reshape(b * seq, A_HEADS * V_HEAD)


ROW_TILE = (D_MODEL // 128, 128)


def _store_row_tiles(ref, rows):
    n = rows.shape[0]
    for s in range(ROW_TILE[0]):
        ref[pl.ds(s, n, stride=ROW_TILE[0]), :] = rows[:, s * 128:(s + 1) * 128]


def _load_row_tiles(ref, n):
    return [ref[pl.ds(s, n, stride=ROW_TILE[0]), :] for s in range(ROW_TILE[0])]


def _merge_kernel(hg_ref, ov_ref, gab_ref, x_ref, wa_ref, wb_ref, wo_ref, gf_ref, wrh_ref, wrl_ref, br_ref,
                  x1_ref, x2_ref, ti_ref, tg_ref, tr_ref, cnt_ref, carry_ref):
    i = pl.program_id(0)

    @pl.when(i == 0)
    def _():
        carry_ref[...] = jnp.zeros_like(carry_ref)

    tm = x_ref.shape[0]
    ya = jnp.dot(hg_ref[...], wa_ref[...], preferred_element_type=F32)
    yb = jnp.dot(ov_ref[...], wb_ref[...], preferred_element_type=F32)
    mixed = (_sigmoid(gab_ref[:, 0:D_MODEL].astype(F32)) * ya
             + _sigmoid(gab_ref[:, D_MODEL:2 * D_MODEL].astype(F32)) * yb)
    x1 = x_ref[...] + jnp.dot(mixed.astype(BF16), wo_ref[...], preferred_element_type=F32)
    x1_ref[...] = x1
    x2 = x1 * lax.rsqrt(jnp.mean(x1 * x1, axis=-1, keepdims=True) + EPS) * gf_ref[...]
    _store_row_tiles(x2_ref, x2)

    lane = lax.broadcasted_iota(jnp.int32, (tm, 128), 1)
    lane_f = lane.astype(F32)
    x2_hi = x2.astype(BF16)
    x2_lo = (x2 - x2_hi.astype(F32)).astype(BF16)
    logits = (jnp.dot(x2_hi, wrh_ref[...], preferred_element_type=F32)
              + jnp.dot(x2_lo, wrh_ref[...], preferred_element_type=F32)
              + jnp.dot(x2_hi, wrl_ref[...], preferred_element_type=F32) + br_ref[...])
    cur = jnp.where(lane < N_EXPERTS, logits, -jnp.inf)
    vals, idxs = [], []
    for _ in range(TOP_K):
        mx = jnp.max(cur, axis=-1, keepdims=True)
        idx = jnp.min(jnp.where(cur == mx, lane_f, 128.0), axis=-1, keepdims=True)
        vals.append(mx)
        idxs.append(idx)
        cur = jnp.where(lane_f == idx, -jnp.inf, cur)
    es = [jnp.exp(v - vals[0]) for v in vals]
    tot = es[0] + es[1] + es[2] + es[3]

    onehots = [(lane_f == idx) for idx in idxs]
    cnt = jnp.zeros((tm, 128), F32)
    for oh in onehots:
        cnt = cnt + jnp.where(oh, 1.0, 0.0)
    rr = lax.broadcasted_iota(jnp.int32, (tm, tm), 0)
    cc = lax.broadcasted_iota(jnp.int32, (tm, tm), 1)
    before = jnp.dot((rr > cc).astype(BF16), cnt.astype(BF16), preferred_element_type=F32) + carry_ref[0:1, :]
    ti = jnp.zeros((tm, 128), F32)
    tg = jnp.zeros((tm, 128), F32)
    tr = jnp.zeros((tm, 128), F32)
    for k in range(TOP_K):
        rank = jnp.sum(jnp.where(onehots[k], before, 0.0), axis=-1, keepdims=True)
        ti = jnp.where(lane == k, idxs[k], ti)
        tg = jnp.where(lane == k, es[k] / tot, tg)
        tr = jnp.where(lane == k, rank, tr)
    ti_ref[...] = ti.astype(jnp.int32)
    tg_ref[...] = tg
    tr_ref[...] = tr.astype(jnp.int32)
    new_carry = carry_ref[0:1, :] + jnp.sum(cnt, axis=0, keepdims=True)
    carry_ref[...] = jnp.broadcast_to(new_carry, carry_ref.shape)
    cnt_ref[...] = jnp.broadcast_to(new_carry, cnt_ref.shape).astype(jnp.int32)


def _merge_route(hg, ov, gab, x2d, wa, wb, wo, g_ffn, w_router_hi, w_router_lo, b_router):
    m = x2d.shape[0]
    tm = min(512, m)
    row = lambda w: pl.BlockSpec((tm, w), lambda i: (i, 0))
    sq = _resident((D_MODEL, D_MODEL))
    return pl.pallas_call(
        _merge_kernel,
        out_shape=(jax.ShapeDtypeStruct((m, D_MODEL), F32), jax.ShapeDtypeStruct((m * ROW_TILE[0], 128), F32),
                   jax.ShapeDtypeStruct((m, 128), jnp.int32), jax.ShapeDtypeStruct((m, 128), F32),
                   jax.ShapeDtypeStruct((m, 128), jnp.int32), jax.ShapeDtypeStruct((8, 128), jnp.int32)),
        grid=(m // tm,),
        in_specs=[row(M_V), row(A_HEADS * V_HEAD), row(W_G), row(D_MODEL), sq, sq, sq,
                  _resident((1, D_MODEL)), _resident((D_MODEL, 128)), _resident((D_MODEL, 128)),
                  _resident((1, 128))],
        out_specs=(row(D_MODEL), pl.BlockSpec((tm * ROW_TILE[0], 128), lambda i: (i, 0)), row(128), row(128),
                   row(128), pl.BlockSpec((8, 128), lambda i: (0, 0))),
        scratch_shapes=[pltpu.VMEM((8, 128), F32)],
        compiler_params=_cparams(("arbitrary",)),
        name="merge_route",
    )(hg, ov, gab, x2d, wa, wb, wo, g_ffn, w_router_hi, w_router_lo, b_router)


def _dispatch_kernel(zt_ref, dest_ref, x_ref, xs_ref, zbuf_ref, sem, zsem):
    i = pl.program_id(0)
    tr = x_ref.shape[0]
    tm = zbuf_ref.shape[0]

    @pl.when(i == 0)
    def _():
        zbuf_ref[...] = jnp.zeros_like(zbuf_ref)
        for e in range(2 * N_EXPERTS):
            @pl.when(zt_ref[e] >= 0)
            def _():
                start = pl.multiple_of(zt_ref[e], tm)
                pltpu.make_async_copy(zbuf_ref, xs_ref.at[pl.ds(start, tm)], zsem).start()
        for e in range(2 * N_EXPERTS):
            @pl.when(zt_ref[e] >= 0)
            def _():
                pltpu.make_async_copy(zbuf_ref, xs_ref.at[pl.ds(0, tm)], zsem).wait()

    def issue(r, carry):
        for k in range(TOP_K):
            pltpu.make_async_copy(x_ref.at[r], xs_ref.at[dest_ref[0, 0, r * TOP_K + k]], sem).start()
        return carry

    lax.fori_loop(0, tr, issue, 0, unroll=8)
    for _ in range(TOP_K):
        pltpu.make_async_copy(x_ref, xs_ref.at[pl.ds(0, tr)], sem).wait()


def _dispatch(x2t, dest, zero_tiles, n_rows):
    m = x2t.shape[0] // ROW_TILE[0]
    tr = min(256, m)
    nt = m // tr
    dest3 = dest.reshape(nt, 1, tr * TOP_K)
    grid_spec = pltpu.PrefetchScalarGridSpec(
        num_scalar_prefetch=1,
        grid=(nt,),
        in_specs=[pl.BlockSpec((1, 1, tr * TOP_K), lambda i, zt: (i, 0, 0), memory_space=pltpu.SMEM),
                  pl.BlockSpec((tr,) + ROW_TILE, lambda i, zt: (i, 0, 0))],
        out_specs=pl.BlockSpec(memory_space=pl.ANY),
        scratch_shapes=[pltpu.VMEM((MOE_TILE,) + ROW_TILE, F32), pltpu.SemaphoreType.DMA(()),
                        pltpu.SemaphoreType.DMA(())],
    )
    xs = pl.pallas_call(
        _dispatch_kernel,
        out_shape=jax.ShapeDtypeStruct((n_rows,) + ROW_TILE, F32),
        grid_spec=grid_spec,
        compiler_params=_cparams(("arbitrary",)),
        name="moe_dispatch",
    )(zero_tiles, dest3, x2t.reshape((m,) + ROW_TILE))
    return xs.reshape(n_rows * ROW_TILE[0], 128)


def _expert_kernel(te_ref, nu_ref, xs_ref, wgu_ref, bgu_ref, wdn_ref, bdn_ref, ys_ref, wgu_b, wdn_b):
    i = pl.program_id(0)
    prev = te_ref[jnp.maximum(i - 1, 0)]

    @pl.when(i < nu_ref[0])
    def _():
        @pl.when((i == 0) | (te_ref[i] != prev))
        def _():
            rc = 128
            for c in range(D_MODEL // rc):
                wgu_b[c * rc:(c + 1) * rc, :] = wgu_ref[c * rc:(c + 1) * rc, :].astype(BF16)
                wdn_b[c * rc:(c + 1) * rc, :] = wdn_ref[c * rc:(c + 1) * rc, :].astype(BF16)

        tm = xs_ref.shape[0] // ROW_TILE[0]
        xb = jnp.concatenate([c.astype(BF16) for c in _load_row_tiles(xs_ref, tm)], axis=1)
        fc = 512
        y = jnp.zeros((tm, D_MODEL), F32) + bdn_ref[...]
        for f0 in range(0, D_FF, fc):
            def proj(c0):
                return jnp.dot(xb, wgu_b[:, c0:c0 + fc], preferred_element_type=F32) + bgu_ref[:, c0:c0 + fc]
            gate = jnp.minimum(proj(f0), SWIGLU_LIMIT)
            up = jnp.clip(proj(D_FF + f0), -SWIGLU_LIMIT, SWIGLU_LIMIT)
            hid = (up + 1.0) * gate * _sigmoid(SWIGLU_ALPHA * gate)
            y = y + jnp.dot(hid.astype(BF16), wdn_b[f0:f0 + fc, :], preferred_element_type=F32)
        _store_row_tiles(ys_ref, y)

    @pl.when(i >= nu_ref[0])
    def _():
        ys_ref[...] = jnp.zeros_like(ys_ref)


def _experts(xs, tile_expert, n_used, w_gu, b_gu, w_dn, b_dn):
    n_rows, w = xs.shape
    tm = MOE_TILE * ROW_TILE[0]
    nt = n_rows // tm
    tile = lambda i, te, nu: (jnp.minimum(i, jnp.maximum(nu[0] - 1, 0)), 0)
    grid_spec = pltpu.PrefetchScalarGridSpec(
        num_scalar_prefetch=2,
        grid=(nt,),
        in_specs=[pl.BlockSpec((tm, w), tile),
                  pl.BlockSpec((None, D_MODEL, 2 * D_FF), lambda i, te, nu: (te[i], 0, 0)),
                  pl.BlockSpec((None, 1, 2 * D_FF), lambda i, te, nu: (te[i], 0, 0)),
                  pl.BlockSpec((None, D_FF, D_MODEL), lambda i, te, nu: (te[i], 0, 0)),
                  pl.BlockSpec((None, 1, D_MODEL), lambda i, te, nu: (te[i], 0, 0))],
        out_specs=pl.BlockSpec((tm, w), lambda i, te, nu: (i, 0)),
        scratch_shapes=[pltpu.VMEM((D_MODEL, 2 * D_FF), BF16), pltpu.VMEM((D_FF, D_MODEL), BF16)],
    )
    return pl.pallas_call(
        _expert_kernel,
        out_shape=jax.ShapeDtypeStruct((n_rows, w), F32),
        grid_spec=grid_spec,
        compiler_params=_cparams(("arbitrary",)),
        name="moe_experts",
    )(tile_expert, n_used, xs, w_gu, b_gu, w_dn, b_dn)


def _combine_kernel(dcur_ref, dnext_ref, x1_ref, tg_ref, gfin_ref, ys_ref, y_ref, buf_ref, sem_ref, *, nt):
    i = pl.program_id(0)
    tc = x1_ref.shape[0]

    def issue(dref, slot):
        def body(r, carry):
            for k in range(TOP_K):
                src0 = pl.multiple_of(dref[0, 0, r * TOP_K + k] * ROW_TILE[0], ROW_TILE[0])
                dst0 = pl.multiple_of(r * ROW_TILE[0], ROW_TILE[0])
                pltpu.make_async_copy(ys_ref.at[pl.ds(src0, ROW_TILE[0]), :],
                                      buf_ref.at[slot, k, pl.ds(dst0, ROW_TILE[0]), :], sem_ref.at[slot]).start()
            return carry
        lax.fori_loop(0, tc, body, 0, unroll=8)

    slot = i % 2

    @pl.when(i == 0)
    def _():
        issue(dcur_ref, 0)

    @pl.when(i + 1 < nt)
    def _():
        issue(dnext_ref, 1 - slot)

    for k in range(TOP_K):
        pltpu.make_async_copy(ys_ref.at[pl.ds(0, tc * ROW_TILE[0]), :], buf_ref.at[slot, k],
                              sem_ref.at[slot]).wait()

    gates = [tg_ref[:, k:k + 1] for k in range(TOP_K)]
    chunks = []
    ssq = jnp.zeros((tc, 1), F32)
    for s in range(ROW_TILE[0]):
        xc = x1_ref[:, s * 128:(s + 1) * 128]
        for k in range(TOP_K):
            xc = xc + gates[k] * buf_ref[slot, k, pl.ds(s, tc, stride=ROW_TILE[0]), :]
        chunks.append(xc)
        ssq = ssq + jnp.sum(xc * xc, axis=-1, keepdims=True)
    inv = lax.rsqrt(ssq / D_MODEL + EPS)
    for s in range(ROW_TILE[0]):
        y_ref[:, s * 128:(s + 1) * 128] = chunks[s] * inv * gfin_ref[:, s * 128:(s + 1) * 128]


def _combine(dest, x1, tg, g_final, ys):
    m = x1.shape[0]
    tc = min(256, m)
    nt = m // tc
    dest3 = dest.reshape(nt, 1, tc * TOP_K)
    kern = functools.partial(_combine_kernel, nt=nt)
    smem = lambda f: pl.BlockSpec((1, 1, tc * TOP_K), f, memory_space=pltpu.SMEM)
    return pl.pallas_call(
        kern,
        out_shape=jax.ShapeDtypeStruct((m, D_MODEL), F32),
        grid=(nt,),
        in_specs=[smem(lambda i: (i, 0, 0)), smem(lambda i: (jnp.minimum(i + 1, nt - 1), 0, 0)),
                  pl.BlockSpec((tc, D_MODEL), lambda i: (i, 0)), pl.BlockSpec((tc, 128), lambda i: (i, 0)),
                  _resident((1, D_MODEL)), pl.BlockSpec(memory_space=pl.ANY)],
        out_specs=pl.BlockSpec((tc, D_MODEL), lambda i: (i, 0)),
        scratch_shapes=[pltpu.VMEM((2, TOP_K, tc * ROW_TILE[0], 128), F32), pltpu.SemaphoreType.DMA((2,))],
        compiler_params=_cparams(("arbitrary",)),
        name="moe_combine",
    )(dest3, dest3, x1, tg, g_final, ys)


def _rope_tables(pos):
    half = ROPE_DIM // 2
    freqs = ROPE_THETA ** (-jnp.arange(half, dtype=F32) / half)
    ang = pos.astype(F32)[:, None] * freqs[None, :]
    cos, sin = jnp.cos(ang), jnp.sin(ang)
    return jnp.concatenate([cos, cos], axis=-1), jnp.concatenate([-sin, sin], axis=-1)


def _tile_rows(t, rows):
    return t if t.shape[0] >= rows else jnp.tile(t, (rows // t.shape[0], 1))


def _pad_lanes(t, width):
    return jnp.pad(t, ((0, 0), (0, width - t.shape[1])))


def _moe(x1, x2n, ti, tg, tr, cnt, g_final, w_gu, b_gu, w_dn, b_dn):
    m = x1.shape[0]
    a = m * TOP_K
    tm = MOE_TILE
    n_tiles = -(-a // tm) + N_EXPERTS
    counts = cnt[0, :N_EXPERTS]
    pcounts = (counts + tm - 1) // tm * tm
    pend = jnp.cumsum(pcounts)
    pstart = pend - pcounts
    top_i = ti[:, :TOP_K]
    eids = jnp.arange(N_EXPERTS, dtype=jnp.int32)
    dest = tr[:, :TOP_K] + jnp.sum(jnp.where(top_i[..., None] == eids, pstart, 0), axis=-1)
    n_used = (pend[-1] // tm).astype(jnp.int32)
    tiles = jnp.minimum(jnp.arange(n_tiles, dtype=jnp.int32), n_used - 1) * tm
    tile_expert = jnp.minimum(jnp.sum(pend[None, :] <= tiles[:, None], axis=-1), N_EXPERTS - 1).astype(jnp.int32)
    tail = n_used + jnp.arange(N_EXPERTS, dtype=jnp.int32)
    zero_tiles = jnp.concatenate([jnp.where(pcounts > 0, pend - tm, -1),
                                  jnp.where(tail < n_tiles, tail * tm, -1)]).astype(jnp.int32)
    xs = _dispatch(x2n, dest.astype(jnp.int32), zero_tiles, n_tiles * tm)
    ys = _experts(xs, tile_expert, n_used.reshape(1), w_gu, b_gu, w_dn, b_dn)
    return _combine(dest.astype(jnp.int32), x1, tg, g_final, ys)


def kernel(x_prompt, x_sample, cache_kv_latent, cache_k_rope, state_mlstm_C, state_mlstm_n, state_mlstm_m, state_conv, meta_tokens, g_mix_norm, w_in, b_if, w_conv, b_conv, g_mh_norm, w_proj_a, g_q_norm, g_kv_norm, w_uq, w_uk, w_uv, w_proj_b, w_out, g_ffn_norm, w_router, b_router, w_gate_up, b_gate_up, w_down, b_down, g_final_norm):
    bsz, seq = x_prompt.shape[0], x_prompt.shape[1]
    dbs, dseq = x_sample.shape[0], x_sample.shape[1]
    past = cache_kv_latent.shape[2]
    assert w_in.shape[0] == 1, "single-layer trunk"
    l = 0

    wi = w_in[l]
    o_gate = 2 * M_QK + 2 * M_V
    o_cq = o_gate + 2 * M_HEADS
    o_ckv = o_cq + Q_LORA
    o_kr = o_ckv + KV_LORA
    o_g = o_kr + ROPE_DIM
    swap = np.concatenate([np.arange(ROPE_DIM // 2, ROPE_DIM), np.arange(ROPE_DIM // 2)])
    w_kr = wi[:, o_kr:o_g]
    w_cat = jnp.concatenate([wi[:, :o_gate], _pad_lanes(wi[:, o_gate:o_cq], W_GATE), wi[:, o_cq:o_kr], w_kr,
                             w_kr[:, swap], wi[:, o_g:]], axis=1).astype(BF16)
    bif = _pad_lanes(b_if[l][None, :], W_GATE)
    uq = w_uq[l].reshape(Q_LORA, A_HEADS, NOPE_DIM + ROPE_DIM)
    wn = uq[:, :, :NOPE_DIM].reshape(Q_LORA, A_HEADS * NOPE_DIM).astype(BF16)
    uq_r = uq[:, :, NOPE_DIM:]
    pad_r = lambda t: jnp.pad(t, ((0, 0), (0, 0), (0, 128 - ROPE_DIM))).reshape(Q_LORA, A_HEADS * 128).astype(BF16)
    wr, wrs = pad_r(uq_r), pad_r(uq_r[:, :, swap])
    wuk = jnp.transpose(w_uk[l], (1, 2, 0)).astype(BF16)
    wuv = jnp.transpose(w_uv[l], (1, 0, 2)).astype(BF16)
    wa, wb, wo = w_proj_a[l].astype(BF16), w_proj_b[l].astype(BF16), w_out[l].astype(BF16)
    w_rt = _pad_lanes(w_router[l], 128)
    w_rt_hi = w_rt.astype(BF16)
    w_rt_lo = (w_rt - w_rt_hi.astype(F32)).astype(BF16)
    b_rt = _pad_lanes(b_router[l][None, :], 128)
    g_mix, g_q, g_kv = g_mix_norm[l][None, :], g_q_norm[l][None, :], g_kv_norm[l][None, :]
    g_mh, g_ffn, g_fin = g_mh_norm[l][None, :], g_ffn_norm[l][None, :], g_final_norm[None, :]
    wcv, bcv = w_conv[l], b_conv[l][None, :]
    w_gu, b_gu = w_gate_up[l], b_gate_up[l][:, None, :]
    w_dn, b_dn = w_down[l], b_down[l][:, None, :]

    ct_m, st_m = _rope_tables(jnp.arange(N_META))
    ct_p, st_p = _rope_tables(N_META + jnp.arange(seq))
    ct_s, st_s = _rope_tables(N_META + past + jnp.arange(dseq))

    def stream_tables(ct, st, m_rows):
        rows = max(ct.shape[0], min(512, m_rows))
        return _tile_rows(ct, rows), _tile_rows(st, rows)

    def prefix(lat, kr):
        pad = ((0, 0), (0, PREFIX_PAD - lat.shape[1]), (0, 0))
        return jnp.pad(lat, pad), jnp.pad(kr, pad)

    za_m, zi_m, _, lat_m, kr_m, _, tail_m = _inproj(meta_tokens, N_META, g_mix, w_cat, g_q, g_kv, ct_m, st_m)
    npad = CHUNK - N_META
    za_mp = jnp.concatenate([jnp.zeros((npad, W_A), BF16), za_m], axis=0)
    lane = jnp.arange(W_GATE)
    neutral = jnp.where(lane < M_HEADS, -1e30, jnp.where(lane < 2 * M_HEADS, 1e30, 0.0)).astype(F32)
    zi_mp = jnp.concatenate([jnp.broadcast_to(neutral, (npad, W_GATE)), zi_m], axis=0)
    zeros_state = (jnp.zeros((1, M_HEADS, M_DK, M_DV), F32), jnp.zeros((1, M_HEADS, 1, M_DK), F32),
                   jnp.zeros((1, M_HEADS, 1, 128), F32), jnp.zeros((1, 8, 2 * M_QK), F32))
    _, c_m, n_m, m_m = _mlstm(za_mp, zi_mp, CHUNK, wcv, bcv, bif, g_mh, *zeros_state)

    def layer_tail(x2d, hg, ov, gab):
        x1, x2n, ti, tg, tr, cnt = _merge_route(hg.reshape(x2d.shape[0], M_V), ov, gab, x2d, wa, wb, wo, g_ffn,
                                                w_rt_hi, w_rt_lo, b_rt)
        return _moe(x1, x2n, ti, tg, tr, cnt, g_fin, w_gu, b_gu, w_dn, b_dn)

    xp = x_prompt.reshape(bsz * seq, D_MODEL)
    ctp, stp = stream_tables(ct_p, st_p, bsz * seq)
    za, zi, cqn, lat, kr, gab, tail = _inproj(xp, seq, g_mix, w_cat, g_q, g_kv, ctp, stp)
    hg, c_p, n_p, m_p = _mlstm(za, zi, seq, wcv, bcv, bif, g_mh, c_m, n_m, m_m, tail_m)
    plat, pkr = prefix(lat_m[None], kr_m[None])
    ctq, stq = _pad_lanes(ct_p, 128), _pad_lanes(st_p, 128)
    ov = _attention(cqn, seq, ctq, stq, wn, wr, wrs, wuk, wuv, plat, pkr, N_META,
                    lat.reshape(bsz, seq, KV_LORA), kr.reshape(bsz, seq, ROPE_DIM), True)
    y_prompt = layer_tail(xp, hg, ov, gab).reshape(bsz, seq, D_MODEL)

    xs2 = x_sample.reshape(dbs * dseq, D_MODEL)
    cts, sts = stream_tables(ct_s, st_s, dbs * dseq)
    za, zi, cqn, lat_s, kr_s, gab, tail_s = _inproj(xs2, dseq, g_mix, w_cat, g_q, g_kv, cts, sts)
    cv0 = jnp.pad(state_conv[l], ((0, 0), (8 - (CONV_W - 1), 0), (0, 0)))
    m0 = jnp.broadcast_to(state_mlstm_m[l][:, :, None, None], (dbs, M_HEADS, 1, 128))
    hg, c_s, n_s, m_s = _mlstm(za, zi, dseq, wcv, bcv, bif, g_mh, state_mlstm_C[l],
                               state_mlstm_n[l][:, :, None, :], m0, cv0)
    plat, pkr = prefix(lat_s.reshape(dbs, dseq, KV_LORA), kr_s.reshape(dbs, dseq, ROPE_DIM))
    ov = _attention(cqn, dseq, _pad_lanes(ct_s, 128), _pad_lanes(st_s, 128), wn, wr, wrs, wuk, wuv, plat, pkr,
                    dseq, cache_kv_latent[l], cache_k_rope[l], False)
    y_sample = layer_tail(xs2, hg, ov, gab).reshape(dbs, dseq, D_MODEL)

    def with_meta(meta_rows, frames, width):
        return jnp.concatenate([jnp.broadcast_to(meta_rows[None], (bsz, N_META, width)),
                                frames.reshape(bsz, seq, width)], axis=1)[None]

    p_lat = with_meta(lat_m, lat, KV_LORA)
    p_kr = with_meta(kr_m, kr, ROPE_DIM)
    tail3 = lambda t: t[:, 8 - (CONV_W - 1):, :][None]
    return (y_prompt, y_sample, p_lat, p_kr, c_p[None], n_p[:, :, 0, :][None], m_p[:, :, 0, 0][None], tail3(tail),
            lat_s.reshape(dbs, dseq, KV_LORA)[None], kr_s.reshape(dbs, dseq, ROPE_DIM)[None], c_s[None],
            n_s[:, :, 0, :][None], m_s[:, :, 0, 0][None], tail3(tail_s))
```

```python
import functools

import jax
import jax.numpy as jnp
import numpy as np
from jax import lax
from jax.experimental import pallas as pl
from jax.experimental.pallas import tpu as pltpu

F32 = jnp.float32
BF16 = jnp.bfloat16
HIGHEST = lax.Precision.HIGHEST

D_MODEL = 1024
N_META = 16
CHUNK = 64
M_HEADS = 4
M_DK = 256
M_DV = 256
M_QK = M_HEADS * M_DK
M_V = M_HEADS * M_DV
CONV_W = 4
A_HEADS = 8
NOPE_DIM = 128
ROPE_DIM = 64
V_HEAD = 128
Q_LORA = 384
KV_LORA = 256
ROPE_THETA = 10000.0
ATTN_SCALE = (NOPE_DIM + ROPE_DIM) ** -0.5
QUERY_SCALE = ATTN_SCALE * float(np.log2(np.e))
N_EXPERTS = 32
TOP_K = 4
D_FF = 1024
SWIGLU_LIMIT = 7.0
SWIGLU_ALPHA = 1.702
EPS = 1e-6

COL_A = 0
W_A = 2 * M_QK + 2 * M_V
COL_GATE = COL_A + W_A
W_GATE = 128
COL_B = COL_GATE + W_GATE
W_B = Q_LORA + KV_LORA + 2 * ROPE_DIM
COL_G = COL_B + W_B
W_G = 2 * D_MODEL
W_IN_COLS = COL_G + W_G

V7X_VMEM_BYTES = 64 * 2**20
VMEM_LIMIT = V7X_VMEM_BYTES - 8 * 2**20

PREFIX_PAD = 128
MOE_TILE = 512


def _sigmoid(x):
    return 1.0 / (1.0 + jnp.exp(-x))


def _cparams(sem):
    return pltpu.CompilerParams(dimension_semantics=sem, vmem_limit_bytes=VMEM_LIMIT)


def _resident(shape):
    nd = len(shape)
    return pl.BlockSpec(shape, lambda *_: (0,) * nd, pipeline_mode=pl.Buffered(1))


def _inproj_kernel(x_ref, g_ref, w_ref, gq_ref, gkv_ref, ct_ref, st_ref,
                   za_ref, zi_ref, cq_ref, lat_ref, kr_ref, gab_ref, tail_ref, xn_ref, *, seq, spt):
    x = x_ref[...]
    xn = x * lax.rsqrt(jnp.mean(x * x, axis=-1, keepdims=True) + EPS) * g_ref[...]
    xn_ref[...] = xn.astype(BF16)
    xb = xn_ref[...]
    rows = x.shape[0]
    for c in range(W_A // 1024):
        acc = jnp.dot(xb, w_ref[:, COL_A + c * 1024:COL_A + (c + 1) * 1024], preferred_element_type=F32)
        za_ref[:, c * 1024:(c + 1) * 1024] = acc.astype(BF16)
        if c * 1024 < 2 * M_QK:
            if spt == 1:
                tail_ref[0, :, c * 1024:(c + 1) * 1024] = acc[rows - 8:rows, :]
            else:
                for s in range(spt):
                    tail_ref[s, :, c * 1024:(c + 1) * 1024] = acc[(s + 1) * seq - 8:(s + 1) * seq, :]
    zi_ref[...] = jnp.dot(xb, w_ref[:, COL_GATE:COL_GATE + W_GATE], preferred_element_type=F32)
    zb = jnp.dot(xb, w_ref[:, COL_B:COL_B + W_B], preferred_element_type=F32)
    cq = zb[:, 0:Q_LORA]
    cq_ref[...] = (cq * lax.rsqrt(jnp.mean(cq * cq, axis=-1, keepdims=True) + EPS) * gq_ref[...]).astype(BF16)
    ckv = zb[:, Q_LORA:Q_LORA + KV_LORA]
    lat_ref[...] = ckv * lax.rsqrt(jnp.mean(ckv * ckv, axis=-1, keepdims=True) + EPS) * gkv_ref[...]
    k_r = zb[:, Q_LORA + KV_LORA:Q_LORA + KV_LORA + ROPE_DIM]
    k_rs = zb[:, Q_LORA + KV_LORA + ROPE_DIM:W_B]
    kr_ref[...] = k_r * ct_ref[...] + k_rs * st_ref[...]
    for c in range(W_G // 1024):
        acc = jnp.dot(xb, w_ref[:, COL_G + c * 1024:COL_G + (c + 1) * 1024], preferred_element_type=F32)
        gab_ref[:, c * 1024:(c + 1) * 1024] = acc.astype(BF16)


def _inproj(x2d, seq, g_mix, w_cat, g_q, g_kv, ctab, stab):
    m = x2d.shape[0]
    nseq = m // seq
    tm = min(512, m)
    spt = max(1, tm // seq)
    tps = max(1, seq // tm)
    nt = m // tm
    kern = functools.partial(_inproj_kernel, seq=seq, spt=spt)
    row = lambda w: pl.BlockSpec((tm, w), lambda i: (i, 0))
    tab = pl.BlockSpec((tm, ROPE_DIM), lambda i: (i % tps, 0))
    if spt == 1:
        tail_spec = pl.BlockSpec((1, 8, 2 * M_QK), lambda i: (i // tps, 0, 0))
    else:
        tail_spec = pl.BlockSpec((spt, 8, 2 * M_QK), lambda i: (i, 0, 0))
    return pl.pallas_call(
        kern,
        out_shape=(jax.ShapeDtypeStruct((m, W_A), BF16), jax.ShapeDtypeStruct((m, W_GATE), F32),
                   jax.ShapeDtypeStruct((m, Q_LORA), BF16), jax.ShapeDtypeStruct((m, KV_LORA), F32),
                   jax.ShapeDtypeStruct((m, ROPE_DIM), F32), jax.ShapeDtypeStruct((m, W_G), BF16),
                   jax.ShapeDtypeStruct((nseq, 8, 2 * M_QK), F32)),
        grid=(nt,),
        in_specs=[row(D_MODEL), _resident((1, D_MODEL)), _resident((D_MODEL, W_IN_COLS)),
                  _resident((1, Q_LORA)), _resident((1, KV_LORA)), tab, tab],
        out_specs=(row(W_A), row(W_GATE), row(Q_LORA), row(KV_LORA), row(ROPE_DIM), row(W_G), tail_spec),
        scratch_shapes=[pltpu.VMEM((tm, D_MODEL), BF16)],
        compiler_params=_cparams(("arbitrary",)),
        name="inproj",
    )(x2d, g_mix, w_cat, g_q, g_kv, ctab, stab)


def _mlstm_kernel(za_ref, zi_ref, wc_ref, bc_ref, bif_ref, gmh_ref, c0_ref, n0_ref, m0_ref, cv0_ref,
                  hg_ref, cout_ref, nout_ref, mout_ref, cs_ref, ns_ref, ms_ref, xbuf_ref, *, L, nc):
    c = pl.program_id(1)

    @pl.when(c == 0)
    def _():
        cs_ref[...] = c0_ref[...]
        ns_ref[...] = n0_ref[...]
        ms_ref[...] = m0_ref[...]
        xbuf_ref[0:8, :] = cv0_ref[...]

    xbuf_ref[8:8 + L, :] = za_ref[:, 0:2 * M_QK].astype(F32)

    g = zi_ref[...] + bif_ref[...]
    lf = jnp.minimum(g, 0.0) - jnp.log(1.0 + jnp.exp(-jnp.abs(g)))
    row = lax.broadcasted_iota(jnp.int32, (L, L), 0)
    col = lax.broadcasted_iota(jnp.int32, (L, L), 1)
    causal = row >= col
    tdims = (((1,), (1,)), ((), ()))

    def dot01(a01, x, dims):
        a = a01.astype(BF16)
        x_hi = x.astype(BF16)
        x_lo = (x - x_hi.astype(F32)).astype(BF16)
        return (lax.dot_general(a, x_hi, dims, preferred_element_type=F32)
                + lax.dot_general(a, x_lo, dims, preferred_element_type=F32))

    bcum = dot01(causal, lf, (((1,), (0,)), ((), ())))
    sub8 = lax.broadcasted_iota(jnp.int32, (8, 128), 0)
    lane8 = lax.broadcasted_iota(jnp.int32, (8, 128), 1)
    i_rows = dot01(lane8 == sub8, g, tdims)
    b_rows = dot01(lane8 == sub8 + M_HEADS, bcum, tdims)

    def conv_silu(col0):
        acc = bc_ref[:, col0:col0 + M_DK] + xbuf_ref[8:8 + L, col0:col0 + M_DK] * wc_ref[3:4, col0:col0 + M_DK]
        for j in range(CONV_W - 1):
            acc = acc + xbuf_ref[5 + j:5 + j + L, col0:col0 + M_DK] * wc_ref[j:j + 1, col0:col0 + M_DK]
        return acc * _sigmoid(acc)

    for h in range(M_HEADS):
        q = conv_silu(h * M_DK)
        k = conv_silu(M_QK + h * M_DK) * (M_DK ** -0.5)
        v = za_ref[:, 2 * M_QK + h * M_DV:2 * M_QK + (h + 1) * M_DV]
        o = za_ref[:, 2 * M_QK + M_V + h * M_DV:2 * M_QK + M_V + (h + 1) * M_DV].astype(F32)
        i_row = i_rows[h:h + 1, :]
        b_row = b_rows[h:h + 1, :]
        i_col = g[:, h:h + 1]
        b_col = bcum[:, M_HEADS + h:M_HEADS + h + 1]
        m_prev = ms_ref[h][:, 0:1]
        dmat = jnp.where(causal, b_col - b_row + i_row, -jnp.inf)
        inter = b_col + m_prev
        m_t = jnp.maximum(inter, jnp.max(dmat, axis=-1, keepdims=True))
        w_intra = jnp.exp(dmat - m_t)
        w_inter = jnp.exp(inter - m_t)
        qb = q.astype(BF16)
        kb = k.astype(BF16)
        s = lax.dot_general(qb, kb, tdims, preferred_element_type=F32) * w_intra
        c_old = cs_ref[h]
        n_old = ns_ref[h]
        num = (jnp.dot(s.astype(BF16), v, preferred_element_type=F32)
               + w_inter * jnp.dot(qb, c_old.astype(BF16), preferred_element_type=F32))
        qn = jnp.sum(s, axis=-1, keepdims=True) + w_inter * jnp.sum(q * n_old, axis=-1, keepdims=True)
        den = jnp.maximum(jnp.abs(qn), jnp.exp(-m_t))
        hh = num / den
        m_new = m_t[L - 1:L, :]
        b_last = bcum[L - 1:L, M_HEADS + h:M_HEADS + h + 1]
        w_prev = jnp.exp(b_last + m_prev - m_new)
        kw = k * jnp.exp(b_last - b_col + i_col - m_new)
        cs_ref[h] = w_prev * c_old + lax.dot_general(kw.astype(BF16), v, (((0,), (0,)), ((), ())),
                                                     preferred_element_type=F32)
        ns_ref[h] = w_prev * n_old + jnp.sum(kw, axis=0, keepdims=True)
        ms_ref[h] = jnp.broadcast_to(m_new, (1, 128))
        hn = hh * lax.rsqrt(jnp.mean(hh * hh, axis=-1, keepdims=True) + EPS) * gmh_ref[:, h * M_DV:(h + 1) * M_DV]
        hg_ref[:, h * M_DV:(h + 1) * M_DV] = (hn * _sigmoid(o)).astype(BF16)

    xbuf_ref[0:8, :] = xbuf_ref[L:L + 8, :]

    @pl.when(c == nc - 1)
    def _():
        cout_ref[...] = cs_ref[...]
        nout_ref[...] = ns_ref[...]
        mout_ref[...] = ms_ref[...]


def _mlstm(za, zi, seq, w_conv, b_conv, bif, g_mh, c0, n0, m0, cv0):
    b = za.shape[0] // seq
    L = min(seq, 256)
    nc = seq // L
    za3 = za.reshape(b, seq, W_A)
    zi3 = zi.reshape(b, seq, W_GATE)
    shared = c0.shape[0] == 1
    bsel = (lambda bi: 0) if shared else (lambda bi: bi)
    kern = functools.partial(_mlstm_kernel, L=L, nc=nc)
    st4 = lambda last2: pl.BlockSpec((None, M_HEADS) + last2, lambda bi, ci: (bsel(bi), 0, 0, 0))
    out4 = lambda last2: pl.BlockSpec((None, M_HEADS) + last2, lambda bi, ci: (bi, 0, 0, 0))
    return pl.pallas_call(
        kern,
        out_shape=(jax.ShapeDtypeStruct((b, seq, M_V), BF16),
                   jax.ShapeDtypeStruct((b, M_HEADS, M_DK, M_DV), F32),
                   jax.ShapeDtypeStruct((b, M_HEADS, 1, M_DK), F32),
                   jax.ShapeDtypeStruct((b, M_HEADS, 1, 128), F32)),
        grid=(b, nc),
        in_specs=[pl.BlockSpec((None, L, W_A), lambda bi, ci: (bi, ci, 0)),
                  pl.BlockSpec((None, L, W_GATE), lambda bi, ci: (bi, ci, 0)),
                  _resident((CONV_W, 2 * M_QK)), _resident((1, 2 * M_QK)), _resident((1, W_GATE)),
                  _resident((1, M_V)),
                  st4((M_DK, M_DV)), st4((1, M_DK)), st4((1, 128)),
                  pl.BlockSpec((None, 8, 2 * M_QK), lambda bi, ci: (bsel(bi), 0, 0))],
        out_specs=(pl.BlockSpec((None, L, M_V), lambda bi, ci: (bi, ci, 0)),
                   out4((M_DK, M_DV)), out4((1, M_DK)), out4((1, 128))),
        scratch_shapes=[pltpu.VMEM((M_HEADS, M_DK, M_DV), F32), pltpu.VMEM((M_HEADS, 1, M_DK), F32),
                        pltpu.VMEM((M_HEADS, 1, 128), F32), pltpu.VMEM((L + 8, 2 * M_QK), F32)],
        compiler_params=_cparams(("parallel", "arbitrary")),
        name="mlstm",
    )(za3, zi3, w_conv, b_conv, bif, g_mh, c0, n0, m0, cv0)


def _attn_kernel(cq_ref, ct_ref, st_ref, wn_ref, wr_ref, wrs_ref, wuk_ref, wuv_ref,
                 plat_ref, pkr_ref, klat_ref, kkr_ref, o_ref,
                 ql_ref, qr_ref, s0_ref, s_ref, mrun_ref, mb_ref, lrun_ref, acc_ref,
                 *, tq, tk, n_prefix, n_kt, causal):
    i = pl.program_id(1)
    r = A_HEADS * tq
    tdims = (((1,), (1,)), ((), ()))
    lane_chunks = tk // 128

    cq = cq_ref[...]
    qn_all = jnp.dot(cq, wn_ref[...], preferred_element_type=F32)
    qr_all = jnp.dot(cq, wr_ref[...], preferred_element_type=F32)
    qrs_all = jnp.dot(cq, wrs_ref[...], preferred_element_type=F32)
    ct = ct_ref[...]
    st = st_ref[...]
    for h in range(A_HEADS):
        qn = qn_all[:, h * NOPE_DIM:(h + 1) * NOPE_DIM].astype(BF16)
        ql = jnp.dot(qn, wuk_ref[h], preferred_element_type=F32) * QUERY_SCALE
        ql_ref[h * tq:(h + 1) * tq, :] = ql.astype(BF16)
        qr = (qr_all[:, h * 128:(h + 1) * 128] * ct + qrs_all[:, h * 128:(h + 1) * 128] * st) * QUERY_SCALE
        qr_ref[h * tq:(h + 1) * tq, :] = qr.astype(BF16)

    def scores(kl, kk):
        return (lax.dot_general(ql_ref[...], kl, tdims, preferred_element_type=F32)
                + lax.dot_general(qr_ref[:, 0:ROPE_DIM], kk, tdims, preferred_element_type=F32))

    pl_b = plat_ref[...].astype(BF16)
    s0 = scores(pl_b, pkr_ref[...].astype(BF16))
    pcol = lax.broadcasted_iota(jnp.int32, s0.shape, 1)
    s0 = jnp.where(pcol < n_prefix, s0, -jnp.inf)
    s0_ref[...] = s0
    mrun_ref[...] = s0

    def score_block(j, masked):
        start = pl.multiple_of(j * tk, tk)
        kl = klat_ref[pl.ds(start, tk), :].astype(BF16)
        kk = kkr_ref[pl.ds(start, tk), :].astype(BF16)
        s = scores(kl, kk)
        if masked:
            qpos = i * tq + (lax.broadcasted_iota(jnp.int32, s.shape, 0) & (tq - 1))
            kpos = start + lax.broadcasted_iota(jnp.int32, s.shape, 1)
            shift = CHUNK.bit_length() - 1
            s = jnp.where(jnp.right_shift(kpos, shift) <= jnp.right_shift(qpos, shift), s, -jnp.inf)
        s_ref[j] = s
        mr = mrun_ref[...]
        for c in range(lane_chunks):
            mr = jnp.maximum(mr, s[:, c * 128:(c + 1) * 128])
        mrun_ref[...] = mr

    def score_body(j, carry):
        score_block(j, False)
        return carry

    if causal:
        last = (i * tq) // tk
        lax.fori_loop(0, last, score_body, 0)
        score_block(last, True)
        n_blocks = last + 1
    else:
        lax.fori_loop(0, n_kt, score_body, 0)
        n_blocks = n_kt

    mb = jnp.broadcast_to(jnp.max(mrun_ref[...], axis=-1, keepdims=True), (r, 128))
    mb_ref[...] = mb
    p0 = jnp.exp2(s0_ref[...] - mb)
    lrun_ref[...] = p0
    acc_ref[...] = jnp.dot(p0.astype(BF16), pl_b, preferred_element_type=F32)

    def value_body(j, carry):
        start = pl.multiple_of(j * tk, tk)
        kl = klat_ref[pl.ds(start, tk), :].astype(BF16)
        mbw = jnp.concatenate([mb_ref[...]] * lane_chunks, axis=1)
        p = jnp.exp2(s_ref[j] - mbw)
        lr = lrun_ref[...]
        for c in range(lane_chunks):
            lr = lr + p[:, c * 128:(c + 1) * 128]
        lrun_ref[...] = lr
        acc_ref[...] += jnp.dot(p.astype(BF16), kl, preferred_element_type=F32)
        return carry

    lax.fori_loop(0, n_blocks, value_body, 0)

    o = acc_ref[...] / jnp.sum(lrun_ref[...], axis=-1, keepdims=True)
    for h in range(A_HEADS):
        oh = o[h * tq:(h + 1) * tq, :].astype(BF16)
        o_ref[:, h * V_HEAD:(h + 1) * V_HEAD] = jnp.dot(oh, wuv_ref[h], preferred_element_type=F32).astype(BF16)


def _attention(cqn, seq, ctab, stab, wn, wr, wrs, wuk, wuv, plat, pkr, n_prefix, klat, kkr, causal):
    b = cqn.shape[0] // seq
    tq = min(seq, 128)
    nq = seq // tq
    tkeys = klat.shape[1]
    tk = 512
    n_kt = tkeys // tk
    r = A_HEADS * tq
    cq3 = cqn.reshape(b, seq, Q_LORA)
    psel = (lambda bi: 0) if plat.shape[0] == 1 else (lambda bi: bi)
    kern = functools.partial(_attn_kernel, tq=tq, tk=tk, n_prefix=n_prefix, n_kt=n_kt, causal=causal)
    tab = pl.BlockSpec((tq, 128), lambda bi, qi: (qi, 0))
    out = pl.pallas_call(
        kern,
        out_shape=jax.ShapeDtypeStruct((b, seq, A_HEADS * V_HEAD), BF16),
        grid=(b, nq),
        in_specs=[pl.BlockSpec((None, tq, Q_LORA), lambda bi, qi: (bi, qi, 0)), tab, tab,
                  _resident((Q_LORA, A_HEADS * NOPE_DIM)), _resident((Q_LORA, A_HEADS * 128)),
                  _resident((Q_LORA, A_HEADS * 128)), _resident((A_HEADS, NOPE_DIM, KV_LORA)),
                  _resident((A_HEADS, KV_LORA, V_HEAD)),
                  pl.BlockSpec((None, PREFIX_PAD, KV_LORA), lambda bi, qi: (psel(bi), 0, 0)),
                  pl.BlockSpec((None, PREFIX_PAD, ROPE_DIM), lambda bi, qi: (psel(bi), 0, 0)),
                  pl.BlockSpec((None, tkeys, KV_LORA), lambda bi, qi: (bi, 0, 0)),
                  pl.BlockSpec((None, tkeys, ROPE_DIM), lambda bi, qi: (bi, 0, 0))],
        out_specs=pl.BlockSpec((None, tq, A_HEADS * V_HEAD), lambda bi, qi: (bi, qi, 0)),
        scratch_shapes=[pltpu.VMEM((r, KV_LORA), BF16), pltpu.VMEM((r, 128), BF16),
                        pltpu.VMEM((r, PREFIX_PAD), F32), pltpu.VMEM((n_kt, r, tk), F32),
                        pltpu.VMEM((r, 128), F32), pltpu.VMEM((r, 128), F32), pltpu.VMEM((r, 128), F32),
                        pltpu.VMEM((r, KV_LORA), F32)],
        compiler_params=_cparams(("parallel", "arbitrary")),
        name="attention",
    )(cq3, ctab, stab, wn, wr, wrs, wuk, wuv, plat, pkr, klat, kkr)
    return out.reshape(b * seq, A_HEADS * V_HEAD)


ROW_TILE = (D_MODEL // 128, 128)


def _store_row_tiles(ref, rows):
    n = rows.shape[0]
    for s in range(ROW_TILE[0]):
        ref[pl.ds(s, n, stride=ROW_TILE[0]), :] = rows[:, s * 128:(s + 1) * 128]


def _load_row_tiles(ref, n):
    return [ref[pl.ds(s, n, stride=ROW_TILE[0]), :] for s in range(ROW_TILE[0])]


def _merge_kernel(hg_ref, ov_ref, gab_ref, x_ref, wa_ref, wb_ref, wo_ref, gf_ref, wrh_ref, wrl_ref, br_ref,
                  x1_ref, x2_ref, ti_ref, tg_ref, tr_ref, cnt_ref, carry_ref):
    i = pl.program_id(0)

    @pl.when(i == 0)
    def _():
        carry_ref[...] = jnp.zeros_like(carry_ref)

    tm = x_ref.shape[0]
    ya = jnp.dot(hg_ref[...], wa_ref[...], preferred_element_type=F32)
    yb = jnp.dot(ov_ref[...], wb_ref[...], preferred_element_type=F32)
    mixed = (_sigmoid(gab_ref[:, 0:D_MODEL].astype(F32)) * ya
             + _sigmoid(gab_ref[:, D_MODEL:2 * D_MODEL].astype(F32)) * yb)
    x1 = x_ref[...] + jnp.dot(mixed.astype(BF16), wo_ref[...], preferred_element_type=F32)
    x1_ref[...] = x1
    x2 = x1 * lax.rsqrt(jnp.mean(x1 * x1, axis=-1, keepdims=True) + EPS) * gf_ref[...]
    _store_row_tiles(x2_ref, x2)

    lane = lax.broadcasted_iota(jnp.int32, (tm, 128), 1)
    lane_f = lane.astype(F32)
    x2_hi = x2.astype(BF16)
    x2_lo = (x2 - x2_hi.astype(F32)).astype(BF16)
    logits = (jnp.dot(x2_hi, wrh_ref[...], preferred_element_type=F32)
              + jnp.dot(x2_lo, wrh_ref[...], preferred_element_type=F32)
              + jnp.dot(x2_hi, wrl_ref[...], preferred_element_type=F32) + br_ref[...])
    cur = jnp.where(lane < N_EXPERTS, logits, -jnp.inf)
    vals, idxs = [], []
    for _ in range(TOP_K):
        mx = jnp.max(cur, axis=-1, keepdims=True)
        idx = jnp.min(jnp.where(cur == mx, lane_f, 128.0), axis=-1, keepdims=True)
        vals.append(mx)
        idxs.append(idx)
        cur = jnp.where(lane_f == idx, -jnp.inf, cur)
    es = [jnp.exp(v - vals[0]) for v in vals]
    tot = es[0] + es[1] + es[2] + es[3]

    onehots = [(lane_f == idx) for idx in idxs]
    cnt = jnp.zeros((tm, 128), F32)
    for oh in onehots:
        cnt = cnt + jnp.where(oh, 1.0, 0.0)
    rr = lax.broadcasted_iota(jnp.int32, (tm, tm), 0)
    cc = lax.broadcasted_iota(jnp.int32, (tm, tm), 1)
    before = jnp.dot((rr > cc).astype(BF16), cnt.astype(BF16), preferred_element_type=F32) + carry_ref[0:1, :]
    ti = jnp.zeros((tm, 128), F32)
    tg = jnp.zeros((tm, 128), F32)
    tr = jnp.zeros((tm, 128), F32)
    for k in range(TOP_K):
        rank = jnp.sum(jnp.where(onehots[k], before, 0.0), axis=-1, keepdims=True)
        ti = jnp.where(lane == k, idxs[k], ti)
        tg = jnp.where(lane == k, es[k] / tot, tg)
        tr = jnp.where(lane == k, rank, tr)
    ti_ref[...] = ti.astype(jnp.int32)
    tg_ref[...] = tg
    tr_ref[...] = tr.astype(jnp.int32)
    new_carry = carry_ref[0:1, :] + jnp.sum(cnt, axis=0, keepdims=True)
    carry_ref[...] = jnp.broadcast_to(new_carry, carry_ref.shape)
    cnt_ref[...] = jnp.broadcast_to(new_carry, cnt_ref.shape).astype(jnp.int32)


def _merge_route(hg, ov, gab, x2d, wa, wb, wo, g_ffn, w_router_hi, w_router_lo, b_router):
    m = x2d.shape[0]
    tm = min(512, m)
    row = lambda w: pl.BlockSpec((tm, w), lambda i: (i, 0))
    sq = _resident((D_MODEL, D_MODEL))
    return pl.pallas_call(
        _merge_kernel,
        out_shape=(jax.ShapeDtypeStruct((m, D_MODEL), F32), jax.ShapeDtypeStruct((m * ROW_TILE[0], 128), F32),
                   jax.ShapeDtypeStruct((m, 128), jnp.int32), jax.ShapeDtypeStruct((m, 128), F32),
                   jax.ShapeDtypeStruct((m, 128), jnp.int32), jax.ShapeDtypeStruct((8, 128), jnp.int32)),
        grid=(m // tm,),
        in_specs=[row(M_V), row(A_HEADS * V_HEAD), row(W_G), row(D_MODEL), sq, sq, sq,
                  _resident((1, D_MODEL)), _resident((D_MODEL, 128)), _resident((D_MODEL, 128)),
                  _resident((1, 128))],
        out_specs=(row(D_MODEL), pl.BlockSpec((tm * ROW_TILE[0], 128), lambda i: (i, 0)), row(128), row(128),
                   row(128), pl.BlockSpec((8, 128), lambda i: (0, 0))),
        scratch_shapes=[pltpu.VMEM((8, 128), F32)],
        compiler_params=_cparams(("arbitrary",)),
        name="merge_route",
    )(hg, ov, gab, x2d, wa, wb, wo, g_ffn, w_router_hi, w_router_lo, b_router)


def _dispatch_kernel(zt_ref, dest_ref, x_ref, *refs):
    xs_ref, zbuf_ref, sem, zsem = refs[-4:]
    i = pl.program_id(0)
    tr = x_ref.shape[0]
    tm = zbuf_ref.shape[0]

    @pl.when((i == 0) & (zt_ref[0] > 0))
    def _():
        zbuf_ref[...] = jnp.zeros_like(zbuf_ref)

        def start_zero(j, carry):
            start = pl.multiple_of(zt_ref[1 + j], tm)
            pltpu.make_async_copy(zbuf_ref, xs_ref.at[pl.ds(start, tm)], zsem).start()
            return carry

        def wait_zero(j, carry):
            pltpu.make_async_copy(zbuf_ref, xs_ref.at[pl.ds(0, tm)], zsem).wait()
            return carry

        lax.fori_loop(0, zt_ref[0], start_zero, 0)
        lax.fori_loop(0, zt_ref[0], wait_zero, 0)

    def issue(r, carry):
        for k in range(TOP_K):
            pltpu.make_async_copy(x_ref.at[r], xs_ref.at[dest_ref[0, 0, r * TOP_K + k]],
                                  sem).start(priority=k % 2)
        return carry

    lax.fori_loop(0, tr, issue, 0, unroll=8)
    for _ in range(TOP_K):
        pltpu.make_async_copy(x_ref, xs_ref.at[pl.ds(0, tr)], sem).wait()


def _dispatch(x2t, dest, zero_tiles, n_rows, xs_prev=None):
    m = x2t.shape[0] // ROW_TILE[0]
    tr = min(256, m)
    nt = m // tr
    dest3 = dest.reshape(nt, 1, tr * TOP_K)
    in_specs = [pl.BlockSpec((1, 1, tr * TOP_K), lambda i, zt: (i, 0, 0), memory_space=pltpu.SMEM),
                pl.BlockSpec((tr,) + ROW_TILE, lambda i, zt: (i, 0, 0))]
    args = [zero_tiles, dest3, x2t.reshape((m,) + ROW_TILE)]
    aliases = {}
    if xs_prev is not None:
        in_specs.append(pl.BlockSpec(memory_space=pl.ANY))
        args.append(xs_prev)
        aliases = {3: 0}
    grid_spec = pltpu.PrefetchScalarGridSpec(
        num_scalar_prefetch=1,
        grid=(nt,),
        in_specs=in_specs,
        out_specs=pl.BlockSpec(memory_space=pl.ANY),
        scratch_shapes=[pltpu.VMEM((MOE_TILE,) + ROW_TILE, F32), pltpu.SemaphoreType.DMA(()),
                        pltpu.SemaphoreType.DMA(())],
    )
    return pl.pallas_call(
        _dispatch_kernel,
        out_shape=jax.ShapeDtypeStruct((n_rows,) + ROW_TILE, F32),
        grid_spec=grid_spec,
        input_output_aliases=aliases,
        compiler_params=_cparams(("arbitrary",)),
        name="moe_dispatch",
    )(*args)


def _expert_kernel(te_ref, nu_ref, xs_ref, wgu_ref, bgu_ref, wdn_ref, bdn_ref, ys_ref, wgu_b, wdn_b):
    i = pl.program_id(0)
    prev = te_ref[jnp.maximum(i - 1, 0)]

    @pl.when(i < nu_ref[0])
    def _():
        @pl.when((i == 0) | (te_ref[i] != prev))
        def _():
            rc = 128
            for c in range(D_MODEL // rc):
                wgu_b[c * rc:(c + 1) * rc, :] = wgu_ref[c * rc:(c + 1) * rc, :].astype(BF16)
                wdn_b[c * rc:(c + 1) * rc, :] = wdn_ref[c * rc:(c + 1) * rc, :].astype(BF16)

        tm = xs_ref.shape[0] // ROW_TILE[0]
        xb = jnp.concatenate([c.astype(BF16) for c in _load_row_tiles(xs_ref, tm)], axis=1)
        fc = 512
        y = jnp.zeros((tm, D_MODEL), F32) + bdn_ref[...]
        for f0 in range(0, D_FF, fc):
            def proj(c0):
                return jnp.dot(xb, wgu_b[:, c0:c0 + fc], preferred_element_type=F32) + bgu_ref[:, c0:c0 + fc]
            gate = jnp.minimum(proj(f0), SWIGLU_LIMIT)
            up = jnp.clip(proj(D_FF + f0), -SWIGLU_LIMIT, SWIGLU_LIMIT)
            hid = (up + 1.0) * gate * _sigmoid(SWIGLU_ALPHA * gate)
            y = y + jnp.dot(hid.astype(BF16), wdn_b[f0:f0 + fc, :], preferred_element_type=F32)
        _store_row_tiles(ys_ref, y)

    @pl.when(i >= nu_ref[0])
    def _():
        ys_ref[...] = jnp.zeros_like(ys_ref)


def _experts(xs, tile_expert, n_used, w_gu, b_gu, w_dn, b_dn):
    n_rows, w = xs.shape
    tm = MOE_TILE * ROW_TILE[0]
    nt = n_rows // tm
    tile = lambda i, te, nu: (jnp.minimum(i, jnp.maximum(nu[0] - 1, 0)), 0)
    grid_spec = pltpu.PrefetchScalarGridSpec(
        num_scalar_prefetch=2,
        grid=(nt,),
        in_specs=[pl.BlockSpec((tm, w), tile),
                  pl.BlockSpec((None, D_MODEL, 2 * D_FF), lambda i, te, nu: (te[i], 0, 0)),
                  pl.BlockSpec((None, 1, 2 * D_FF), lambda i, te, nu: (te[i], 0, 0)),
                  pl.BlockSpec((None, D_FF, D_MODEL), lambda i, te, nu: (te[i], 0, 0)),
                  pl.BlockSpec((None, 1, D_MODEL), lambda i, te, nu: (te[i], 0, 0))],
        out_specs=pl.BlockSpec((tm, w), lambda i, te, nu: (i, 0)),
        scratch_shapes=[pltpu.VMEM((D_MODEL, 2 * D_FF), BF16), pltpu.VMEM((D_FF, D_MODEL), BF16)],
    )
    return pl.pallas_call(
        _expert_kernel,
        out_shape=jax.ShapeDtypeStruct((n_rows, w), F32),
        grid_spec=grid_spec,
        compiler_params=_cparams(("arbitrary",)),
        name="moe_experts",
    )(tile_expert, n_used, xs, w_gu, b_gu, w_dn, b_dn)


def _combine_kernel(dcur_ref, dnext_ref, x1_ref, tg_ref, gfin_ref, ys_ref, y_ref, buf_ref, sem_ref, *, nt):
    i = pl.program_id(0)
    tc = x1_ref.shape[0]

    def issue(dref, slot):
        def body(r, carry):
            for k in range(TOP_K):
                src0 = pl.multiple_of(dref[0, 0, r * TOP_K + k] * ROW_TILE[0], ROW_TILE[0])
                dst0 = pl.multiple_of(r * ROW_TILE[0], ROW_TILE[0])
                pltpu.make_async_copy(ys_ref.at[pl.ds(src0, ROW_TILE[0]), :],
                                      buf_ref.at[slot, k, pl.ds(dst0, ROW_TILE[0]), :],
                                      sem_ref.at[slot]).start(priority=k % 2)
            return carry
        lax.fori_loop(0, tc, body, 0, unroll=8)

    slot = i % 2

    @pl.when(i == 0)
    def _():
        issue(dcur_ref, 0)

    @pl.when(i + 1 < nt)
    def _():
        issue(dnext_ref, 1 - slot)

    for k in range(TOP_K):
        pltpu.make_async_copy(ys_ref.at[pl.ds(0, tc * ROW_TILE[0]), :], buf_ref.at[slot, k],
                              sem_ref.at[slot]).wait()

    gates = [tg_ref[:, k:k + 1] for k in range(TOP_K)]
    chunks = []
    ssq = jnp.zeros((tc, 1), F32)
    for s in range(ROW_TILE[0]):
        xc = x1_ref[:, s * 128:(s + 1) * 128]
        for k in range(TOP_K):
            xc = xc + gates[k] * buf_ref[slot, k, pl.ds(s, tc, stride=ROW_TILE[0]), :]
        chunks.append(xc)
        ssq = ssq + jnp.sum(xc * xc, axis=-1, keepdims=True)
    inv = lax.rsqrt(ssq / D_MODEL + EPS)
    for s in range(ROW_TILE[0]):
        y_ref[:, s * 128:(s + 1) * 128] = chunks[s] * inv * gfin_ref[:, s * 128:(s + 1) * 128]


def _combine(dest, x1, tg, g_final, ys):
    m = x1.shape[0]
    tc = min(256, m)
    nt = m // tc
    dest3 = dest.reshape(nt, 1, tc * TOP_K)
    kern = functools.partial(_combine_kernel, nt=nt)
    smem = lambda f: pl.BlockSpec((1, 1, tc * TOP_K), f, memory_space=pltpu.SMEM)
    return pl.pallas_call(
        kern,
        out_shape=jax.ShapeDtypeStruct((m, D_MODEL), F32),
        grid=(nt,),
        in_specs=[smem(lambda i: (i, 0, 0)), smem(lambda i: (jnp.minimum(i + 1, nt - 1), 0, 0)),
                  pl.BlockSpec((tc, D_MODEL), lambda i: (i, 0)), pl.BlockSpec((tc, 128), lambda i: (i, 0)),
                  _resident((1, D_MODEL)), pl.BlockSpec(memory_space=pl.ANY)],
        out_specs=pl.BlockSpec((tc, D_MODEL), lambda i: (i, 0)),
        scratch_shapes=[pltpu.VMEM((2, TOP_K, tc * ROW_TILE[0], 128), F32), pltpu.SemaphoreType.DMA((2,))],
        compiler_params=_cparams(("arbitrary",)),
        name="moe_combine",
    )(dest3, dest3, x1, tg, g_final, ys)


def _rope_tables(pos):
    half = ROPE_DIM // 2
    freqs = ROPE_THETA ** (-jnp.arange(half, dtype=F32) / half)
    ang = pos.astype(F32)[:, None] * freqs[None, :]
    cos, sin = jnp.cos(ang), jnp.sin(ang)
    return jnp.concatenate([cos, cos], axis=-1), jnp.concatenate([-sin, sin], axis=-1)


def _tile_rows(t, rows):
    return t if t.shape[0] >= rows else jnp.tile(t, (rows // t.shape[0], 1))


def _pad_lanes(t, width):
    return jnp.pad(t, ((0, 0), (0, width - t.shape[1])))


def _moe(streams, g_final, w_gu, b_gu, w_dn, b_dn):
    tm = MOE_TILE
    a = sum(s[0].shape[0] for s in streams) * TOP_K
    n_tiles = -(-a // tm) + N_EXPERTS
    counts = [s[5][0, :N_EXPERTS] for s in streams]
    total = sum(counts)
    pcounts = (total + tm - 1) // tm * tm
    pend = jnp.cumsum(pcounts)
    n_used = (pend[-1] // tm).astype(jnp.int32)
    tiles = jnp.minimum(jnp.arange(n_tiles, dtype=jnp.int32), n_used - 1) * tm
    tile_expert = jnp.minimum(jnp.sum(pend[None, :] <= tiles[:, None], axis=-1), N_EXPERTS - 1).astype(jnp.int32)
    eids = jnp.arange(N_EXPERTS, dtype=jnp.int32)
    base = pend - pcounts
    rest_tiles = (a - streams[0][0].shape[0] * TOP_K) // tm
    first = (base + counts[0]) // tm
    cand = first[:, None] + jnp.arange(rest_tiles + 2, dtype=jnp.int32)[None, :]
    cand = jnp.where((cand < (pend // tm)[:, None]) & (pcounts[:, None] > 0), cand, -1)
    tail = n_used + jnp.arange(N_EXPERTS, dtype=jnp.int32)
    cand = jnp.concatenate([cand.reshape(-1), jnp.where(tail < n_tiles, tail, -1)])
    n_zero = jnp.sum(cand >= 0).astype(jnp.int32)
    zlist = (-jnp.sort(-cand))[:rest_tiles + 3 * N_EXPERTS] * tm
    zero_tiles = jnp.concatenate([n_zero[None], zlist]).astype(jnp.int32)
    dests, xs = [], None
    for s, (_, x2t, ti, _, tr, _) in enumerate(streams):
        dest = tr[:, :TOP_K] + jnp.sum(jnp.where(ti[:, :TOP_K, None] == eids, base, 0), axis=-1)
        dests.append(dest.astype(jnp.int32))
        zt = zero_tiles if s == 0 else jnp.zeros_like(zero_tiles)
        xs = _dispatch(x2t, dests[-1], zt, n_tiles * tm, xs)
        base = base + counts[s]
    ys = _experts(xs.reshape(n_tiles * tm * ROW_TILE[0], 128), tile_expert, n_used.reshape(1),
                  w_gu, b_gu, w_dn, b_dn)
    return [_combine(dest, s[0], s[3], g_final, ys) for dest, s in zip(dests, streams)]


def kernel(x_prompt, x_sample, cache_kv_latent, cache_k_rope, state_mlstm_C, state_mlstm_n, state_mlstm_m, state_conv, meta_tokens, g_mix_norm, w_in, b_if, w_conv, b_conv, g_mh_norm, w_proj_a, g_q_norm, g_kv_norm, w_uq, w_uk, w_uv, w_proj_b, w_out, g_ffn_norm, w_router, b_router, w_gate_up, b_gate_up, w_down, b_down, g_final_norm):
    bsz, seq = x_prompt.shape[0], x_prompt.shape[1]
    dbs, dseq = x_sample.shape[0], x_sample.shape[1]
    past = cache_kv_latent.shape[2]
    assert w_in.shape[0] == 1, "single-layer trunk"
    l = 0

    wi = w_in[l]
    o_gate = 2 * M_QK + 2 * M_V
    o_cq = o_gate + 2 * M_HEADS
    o_ckv = o_cq + Q_LORA
    o_kr = o_ckv + KV_LORA
    o_g = o_kr + ROPE_DIM
    swap = np.concatenate([np.arange(ROPE_DIM // 2, ROPE_DIM), np.arange(ROPE_DIM // 2)])
    w_kr = wi[:, o_kr:o_g]
    w_cat = jnp.concatenate([wi[:, :o_gate], _pad_lanes(wi[:, o_gate:o_cq], W_GATE), wi[:, o_cq:o_kr], w_kr,
                             w_kr[:, swap], wi[:, o_g:]], axis=1).astype(BF16)
    bif = _pad_lanes(b_if[l][None, :], W_GATE)
    uq = w_uq[l].reshape(Q_LORA, A_HEADS, NOPE_DIM + ROPE_DIM)
    wn = uq[:, :, :NOPE_DIM].reshape(Q_LORA, A_HEADS * NOPE_DIM).astype(BF16)
    uq_r = uq[:, :, NOPE_DIM:]
    pad_r = lambda t: jnp.pad(t, ((0, 0), (0, 0), (0, 128 - ROPE_DIM))).reshape(Q_LORA, A_HEADS * 128).astype(BF16)
    wr, wrs = pad_r(uq_r), pad_r(uq_r[:, :, swap])
    wuk = jnp.transpose(w_uk[l], (1, 2, 0)).astype(BF16)
    wuv = jnp.transpose(w_uv[l], (1, 0, 2)).astype(BF16)
    wa, wb, wo = w_proj_a[l].astype(BF16), w_proj_b[l].astype(BF16), w_out[l].astype(BF16)
    w_rt = _pad_lanes(w_router[l], 128)
    w_rt_hi = w_rt.astype(BF16)
    w_rt_lo = (w_rt - w_rt_hi.astype(F32)).astype(BF16)
    b_rt = _pad_lanes(b_router[l][None, :], 128)
    g_mix, g_q, g_kv = g_mix_norm[l][None, :], g_q_norm[l][None, :], g_kv_norm[l][None, :]
    g_mh, g_ffn, g_fin = g_mh_norm[l][None, :], g_ffn_norm[l][None, :], g_final_norm[None, :]
    wcv, bcv = w_conv[l], b_conv[l][None, :]
    w_gu, b_gu = w_gate_up[l], b_gate_up[l][:, None, :]
    w_dn, b_dn = w_down[l], b_down[l][:, None, :]

    ct_m, st_m = _rope_tables(jnp.arange(N_META))
    ct_p, st_p = _rope_tables(N_META + jnp.arange(seq))
    ct_s, st_s = _rope_tables(N_META + past + jnp.arange(dseq))

    def stream_tables(ct, st, m_rows):
        rows = max(ct.shape[0], min(512, m_rows))
        return _tile_rows(ct, rows), _tile_rows(st, rows)

    def prefix(lat, kr):
        pad = ((0, 0), (0, PREFIX_PAD - lat.shape[1]), (0, 0))
        return jnp.pad(lat, pad), jnp.pad(kr, pad)

    za_m, zi_m, _, lat_m, kr_m, _, tail_m = _inproj(meta_tokens, N_META, g_mix, w_cat, g_q, g_kv, ct_m, st_m)
    npad = CHUNK - N_META
    za_mp = jnp.concatenate([jnp.zeros((npad, W_A), BF16), za_m], axis=0)
    lane = jnp.arange(W_GATE)
    neutral = jnp.where(lane < M_HEADS, -1e30, jnp.where(lane < 2 * M_HEADS, 1e30, 0.0)).astype(F32)
    zi_mp = jnp.concatenate([jnp.broadcast_to(neutral, (npad, W_GATE)), zi_m], axis=0)
    zeros_state = (jnp.zeros((1, M_HEADS, M_DK, M_DV), F32), jnp.zeros((1, M_HEADS, 1, M_DK), F32),
                   jnp.zeros((1, M_HEADS, 1, 128), F32), jnp.zeros((1, 8, 2 * M_QK), F32))
    _, c_m, n_m, m_m = _mlstm(za_mp, zi_mp, CHUNK, wcv, bcv, bif, g_mh, *zeros_state)

    def route(x2d, hg, ov, gab):
        return _merge_route(hg.reshape(x2d.shape[0], M_V), ov, gab, x2d, wa, wb, wo, g_ffn, w_rt_hi, w_rt_lo, b_rt)

    xp = x_prompt.reshape(bsz * seq, D_MODEL)
    ctp, stp = stream_tables(ct_p, st_p, bsz * seq)
    za, zi, cqn, lat, kr, gab, tail = _inproj(xp, seq, g_mix, w_cat, g_q, g_kv, ctp, stp)
    hg, c_p, n_p, m_p = _mlstm(za, zi, seq, wcv, bcv, bif, g_mh, c_m, n_m, m_m, tail_m)
    plat, pkr = prefix(lat_m[None], kr_m[None])
    ctq, stq = _pad_lanes(ct_p, 128), _pad_lanes(st_p, 128)
    ov = _attention(cqn, seq, ctq, stq, wn, wr, wrs, wuk, wuv, plat, pkr, N_META,
                    lat.reshape(bsz, seq, KV_LORA), kr.reshape(bsz, seq, ROPE_DIM), True)
    routed_p = route(xp, hg, ov, gab)

    xs2 = x_sample.reshape(dbs * dseq, D_MODEL)
    cts, sts = stream_tables(ct_s, st_s, dbs * dseq)
    za, zi, cqn, lat_s, kr_s, gab, tail_s = _inproj(xs2, dseq, g_mix, w_cat, g_q, g_kv, cts, sts)
    cv0 = jnp.pad(state_conv[l], ((0, 0), (8 - (CONV_W - 1), 0), (0, 0)))
    m0 = jnp.broadcast_to(state_mlstm_m[l][:, :, None, None], (dbs, M_HEADS, 1, 128))
    hg, c_s, n_s, m_s = _mlstm(za, zi, dseq, wcv, bcv, bif, g_mh, state_mlstm_C[l],
                               state_mlstm_n[l][:, :, None, :], m0, cv0)
    plat, pkr = prefix(lat_s.reshape(dbs, dseq, KV_LORA), kr_s.reshape(dbs, dseq, ROPE_DIM))
    ov = _attention(cqn, dseq, _pad_lanes(ct_s, 128), _pad_lanes(st_s, 128), wn, wr, wrs, wuk, wuv, plat, pkr,
                    dseq, cache_kv_latent[l], cache_k_rope[l], False)
    routed_s = route(xs2, hg, ov, gab)

    y_p, y_s = _moe([routed_p, routed_s], g_fin, w_gu, b_gu, w_dn, b_dn)
    y_prompt = y_p.reshape(bsz, seq, D_MODEL)
    y_sample = y_s.reshape(dbs, dseq, D_MODEL)

    def with_meta(meta_rows, frames, width):
        return jnp.concatenate([jnp.broadcast_to(meta_rows[None], (bsz, N_META, width)),
                                frames.reshape(bsz, seq, width)], axis=1)[None]

    p_lat = with_meta(lat_m, lat, KV_LORA)
    p_kr = with_meta(kr_m, kr, ROPE_DIM)
    tail3 = lambda t: t[:, 8 - (CONV_W - 1):, :][None]
    return (y_prompt, y_sample, p_lat, p_kr, c_p[None], n_p[:, :, 0, :][None], m_p[:, :, 0, 0][None], tail3(tail),
            lat_s.reshape(dbs, dseq, KV_LORA)[None], kr_s.reshape(dbs, dseq, ROPE_DIM)[None], c_s[None],
            n_s[:, :, 0, :][None], m_s[:, :, 0, 0][None], tail3(tail_s))
```

```python
import functools

import jax
import jax.numpy as jnp
import numpy as np
from jax import lax
from jax.experimental import pallas as pl
from jax.experimental.pallas import tpu as pltpu

F32 = jnp.float32
BF16 = jnp.bfloat16
HIGHEST = lax.Precision.HIGHEST

D_MODEL = 1024
N_META = 16
CHUNK = 64
M_HEADS = 4
M_DK = 256
M_DV = 256
M_QK = M_HEADS * M_DK
M_V = M_HEADS * M_DV
CONV_W = 4
A_HEADS = 8
NOPE_DIM = 128
ROPE_DIM = 64
V_HEAD = 128
Q_LORA = 384
KV_LORA = 256
ROPE_THETA = 10000.0
ATTN_SCALE = (NOPE_DIM + ROPE_DIM) ** -0.5
QUERY_SCALE = ATTN_SCALE * float(np.log2(np.e))
N_EXPERTS = 32
TOP_K = 4
D_FF = 1024
SWIGLU_LIMIT = 7.0
SWIGLU_ALPHA = 1.702
EPS = 1e-6

COL_A = 0
W_A = 2 * M_QK + 2 * M_V
COL_GATE = COL_A + W_A
W_GATE = 128
COL_B = COL_GATE + W_GATE
W_B = Q_LORA + KV_LORA + 2 * ROPE_DIM
COL_G = COL_B + W_B
W_G = 2 * D_MODEL
W_IN_COLS = COL_G + W_G

V7X_VMEM_BYTES = 64 * 2**20
VMEM_LIMIT = V7X_VMEM_BYTES - 8 * 2**20

PREFIX_PAD = 128
MOE_TILE = 512


def _sigmoid(x):
    return 1.0 / (1.0 + jnp.exp(-x))


def _cparams(sem):
    return pltpu.CompilerParams(dimension_semantics=sem, vmem_limit_bytes=VMEM_LIMIT)


def _resident(shape):
    nd = len(shape)
    return pl.BlockSpec(shape, lambda *_: (0,) * nd, pipeline_mode=pl.Buffered(1))


def _inproj_kernel(x_ref, g_ref, w_ref, gq_ref, gkv_ref, ct_ref, st_ref,
                   za_ref, zi_ref, cq_ref, lat_ref, kr_ref, gab_ref, tail_ref, xn_ref, *, seq, spt):
    x = x_ref[...]
    xn = x * lax.rsqrt(jnp.mean(x * x, axis=-1, keepdims=True) + EPS) * g_ref[...]
    xn_ref[...] = xn.astype(BF16)
    xb = xn_ref[...]
    rows = x.shape[0]
    for c in range(W_A // 1024):
        acc = jnp.dot(xb, w_ref[:, COL_A + c * 1024:COL_A + (c + 1) * 1024], preferred_element_type=F32)
        za_ref[:, c * 1024:(c + 1) * 1024] = acc.astype(BF16)
        if c * 1024 < 2 * M_QK:
            if spt == 1:
                tail_ref[0, :, c * 1024:(c + 1) * 1024] = acc[rows - 8:rows, :]
            else:
                for s in range(spt):
                    tail_ref[s, :, c * 1024:(c + 1) * 1024] = acc[(s + 1) * seq - 8:(s + 1) * seq, :]
    zi_ref[...] = jnp.dot(xb, w_ref[:, COL_GATE:COL_GATE + W_GATE], preferred_element_type=F32)
    zb = jnp.dot(xb, w_ref[:, COL_B:COL_B + W_B], preferred_element_type=F32)
    cq = zb[:, 0:Q_LORA]
    cq_ref[...] = (cq * lax.rsqrt(jnp.mean(cq * cq, axis=-1, keepdims=True) + EPS) * gq_ref[...]).astype(BF16)
    ckv = zb[:, Q_LORA:Q_LORA + KV_LORA]
    lat_ref[...] = ckv * lax.rsqrt(jnp.mean(ckv * ckv, axis=-1, keepdims=True) + EPS) * gkv_ref[...]
    k_r = zb[:, Q_LORA + KV_LORA:Q_LORA + KV_LORA + ROPE_DIM]
    k_rs = zb[:, Q_LORA + KV_LORA + ROPE_DIM:W_B]
    kr_ref[...] = k_r * ct_ref[...] + k_rs * st_ref[...]
    for c in range(W_G // 1024):
        acc = jnp.dot(xb, w_ref[:, COL_G + c * 1024:COL_G + (c + 1) * 1024], preferred_element_type=F32)
        gab_ref[:, c * 1024:(c + 1) * 1024] = acc.astype(BF16)


def _inproj(x2d, seq, g_mix, w_cat, g_q, g_kv, ctab, stab):
    m = x2d.shape[0]
    nseq = m // seq
    tm = min(512, m)
    spt = max(1, tm // seq)
    tps = max(1, seq // tm)
    nt = m // tm
    kern = functools.partial(_inproj_kernel, seq=seq, spt=spt)
    row = lambda w: pl.BlockSpec((tm, w), lambda i: (i, 0))
    tab = pl.BlockSpec((tm, ROPE_DIM), lambda i: (i % tps, 0))
    if spt == 1:
        tail_spec = pl.BlockSpec((1, 8, 2 * M_QK), lambda i: (i // tps, 0, 0))
    else:
        tail_spec = pl.BlockSpec((spt, 8, 2 * M_QK), lambda i: (i, 0, 0))
    return pl.pallas_call(
        kern,
        out_shape=(jax.ShapeDtypeStruct((m, W_A), BF16), jax.ShapeDtypeStruct((m, W_GATE), F32),
                   jax.ShapeDtypeStruct((m, Q_LORA), BF16), jax.ShapeDtypeStruct((m, KV_LORA), F32),
                   jax.ShapeDtypeStruct((m, ROPE_DIM), F32), jax.ShapeDtypeStruct((m, W_G), BF16),
                   jax.ShapeDtypeStruct((nseq, 8, 2 * M_QK), F32)),
        grid=(nt,),
        in_specs=[row(D_MODEL), _resident((1, D_MODEL)), _resident((D_MODEL, W_IN_COLS)),
                  _resident((1, Q_LORA)), _resident((1, KV_LORA)), tab, tab],
        out_specs=(row(W_A), row(W_GATE), row(Q_LORA), row(KV_LORA), row(ROPE_DIM), row(W_G), tail_spec),
        scratch_shapes=[pltpu.VMEM((tm, D_MODEL), BF16)],
        compiler_params=_cparams(("arbitrary",)),
        name="inproj",
    )(x2d, g_mix, w_cat, g_q, g_kv, ctab, stab)


def _mlstm_kernel(za_ref, zi_ref, wc_ref, bc_ref, bif_ref, gmh_ref, c0_ref, n0_ref, m0_ref, cv0_ref,
                  hg_ref, cout_ref, nout_ref, mout_ref, cs_ref, ns_ref, ms_ref, xbuf_ref, *, L, nc):
    c = pl.program_id(1)

    @pl.when(c == 0)
    def _():
        cs_ref[...] = c0_ref[...]
        ns_ref[...] = n0_ref[...]
        ms_ref[...] = m0_ref[...]
        xbuf_ref[0:8, :] = cv0_ref[...]

    xbuf_ref[8:8 + L, :] = za_ref[:, 0:2 * M_QK].astype(F32)

    g = zi_ref[...] + bif_ref[...]
    lf = jnp.minimum(g, 0.0) - jnp.log(1.0 + jnp.exp(-jnp.abs(g)))
    row = lax.broadcasted_iota(jnp.int32, (L, L), 0)
    col = lax.broadcasted_iota(jnp.int32, (L, L), 1)
    causal = row >= col
    tdims = (((1,), (1,)), ((), ()))

    def dot01(a01, x, dims):
        a = a01.astype(BF16)
        x_hi = x.astype(BF16)
        x_lo = (x - x_hi.astype(F32)).astype(BF16)
        return (lax.dot_general(a, x_hi, dims, preferred_element_type=F32)
                + lax.dot_general(a, x_lo, dims, preferred_element_type=F32))

    bcum = dot01(causal, lf, (((1,), (0,)), ((), ())))
    sub8 = lax.broadcasted_iota(jnp.int32, (8, 128), 0)
    lane8 = lax.broadcasted_iota(jnp.int32, (8, 128), 1)
    i_rows = dot01(lane8 == sub8, g, tdims)
    b_rows = dot01(lane8 == sub8 + M_HEADS, bcum, tdims)

    def conv_silu(col0):
        acc = bc_ref[:, col0:col0 + M_DK] + xbuf_ref[8:8 + L, col0:col0 + M_DK] * wc_ref[3:4, col0:col0 + M_DK]
        for j in range(CONV_W - 1):
            acc = acc + xbuf_ref[5 + j:5 + j + L, col0:col0 + M_DK] * wc_ref[j:j + 1, col0:col0 + M_DK]
        return acc * _sigmoid(acc)

    for h in range(M_HEADS):
        q = conv_silu(h * M_DK)
        k = conv_silu(M_QK + h * M_DK) * (M_DK ** -0.5)
        v = za_ref[:, 2 * M_QK + h * M_DV:2 * M_QK + (h + 1) * M_DV]
        o = za_ref[:, 2 * M_QK + M_V + h * M_DV:2 * M_QK + M_V + (h + 1) * M_DV].astype(F32)
        i_row = i_rows[h:h + 1, :]
        b_row = b_rows[h:h + 1, :]
        i_col = g[:, h:h + 1]
        b_col = bcum[:, M_HEADS + h:M_HEADS + h + 1]
        m_prev = ms_ref[h][:, 0:1]
        dmat = jnp.where(causal, b_col - b_row + i_row, -jnp.inf)
        inter = b_col + m_prev
        m_t = jnp.maximum(inter, jnp.max(dmat, axis=-1, keepdims=True))
        w_intra = jnp.exp(dmat - m_t)
        w_inter = jnp.exp(inter - m_t)
        qb = q.astype(BF16)
        kb = k.astype(BF16)
        s = lax.dot_general(qb, kb, tdims, preferred_element_type=F32) * w_intra
        c_old = cs_ref[h]
        n_old = ns_ref[h]
        num = (jnp.dot(s.astype(BF16), v, preferred_element_type=F32)
               + w_inter * jnp.dot(qb, c_old.astype(BF16), preferred_element_type=F32))
        qn = jnp.sum(s, axis=-1, keepdims=True) + w_inter * jnp.sum(q * n_old, axis=-1, keepdims=True)
        den = jnp.maximum(jnp.abs(qn), jnp.exp(-m_t))
        hh = num / den
        m_new = m_t[L - 1:L, :]
        b_last = bcum[L - 1:L, M_HEADS + h:M_HEADS + h + 1]
        w_prev = jnp.exp(b_last + m_prev - m_new)
        kw = k * jnp.exp(b_last - b_col + i_col - m_new)
        cs_ref[h] = w_prev * c_old + lax.dot_general(kw.astype(BF16), v, (((0,), (0,)), ((), ())),
                                                     preferred_element_type=F32)
        ns_ref[h] = w_prev * n_old + jnp.sum(kw, axis=0, keepdims=True)
        ms_ref[h] = jnp.broadcast_to(m_new, (1, 128))
        hn = hh * lax.rsqrt(jnp.mean(hh * hh, axis=-1, keepdims=True) + EPS) * gmh_ref[:, h * M_DV:(h + 1) * M_DV]
        hg_ref[:, h * M_DV:(h + 1) * M_DV] = (hn * _sigmoid(o)).astype(BF16)

    xbuf_ref[0:8, :] = xbuf_ref[L:L + 8, :]

    @pl.when(c == nc - 1)
    def _():
        cout_ref[...] = cs_ref[...]
        nout_ref[...] = ns_ref[...]
        mout_ref[...] = ms_ref[...]


def _mlstm(za, zi, seq, w_conv, b_conv, bif, g_mh, c0, n0, m0, cv0):
    b = za.shape[0] // seq
    L = min(seq, 256)
    nc = seq // L
    za3 = za.reshape(b, seq, W_A)
    zi3 = zi.reshape(b, seq, W_GATE)
    shared = c0.shape[0] == 1
    bsel = (lambda bi: 0) if shared else (lambda bi: bi)
    kern = functools.partial(_mlstm_kernel, L=L, nc=nc)
    st4 = lambda last2: pl.BlockSpec((None, M_HEADS) + last2, lambda bi, ci: (bsel(bi), 0, 0, 0))
    out4 = lambda last2: pl.BlockSpec((None, M_HEADS) + last2, lambda bi, ci: (bi, 0, 0, 0))
    return pl.pallas_call(
        kern,
        out_shape=(jax.ShapeDtypeStruct((b, seq, M_V), BF16),
                   jax.ShapeDtypeStruct((b, M_HEADS, M_DK, M_DV), F32),
                   jax.ShapeDtypeStruct((b, M_HEADS, 1, M_DK), F32),
                   jax.ShapeDtypeStruct((b, M_HEADS, 1, 128), F32)),
        grid=(b, nc),
        in_specs=[pl.BlockSpec((None, L, W_A), lambda bi, ci: (bi, ci, 0)),
                  pl.BlockSpec((None, L, W_GATE), lambda bi, ci: (bi, ci, 0)),
                  _resident((CONV_W, 2 * M_QK)), _resident((1, 2 * M_QK)), _resident((1, W_GATE)),
                  _resident((1, M_V)),
                  st4((M_DK, M_DV)), st4((1, M_DK)), st4((1, 128)),
                  pl.BlockSpec((None, 8, 2 * M_QK), lambda bi, ci: (bsel(bi), 0, 0))],
        out_specs=(pl.BlockSpec((None, L, M_V), lambda bi, ci: (bi, ci, 0)),
                   out4((M_DK, M_DV)), out4((1, M_DK)), out4((1, 128))),
        scratch_shapes=[pltpu.VMEM((M_HEADS, M_DK, M_DV), F32), pltpu.VMEM((M_HEADS, 1, M_DK), F32),
                        pltpu.VMEM((M_HEADS, 1, 128), F32), pltpu.VMEM((L + 8, 2 * M_QK), F32)],
        compiler_params=_cparams(("parallel", "arbitrary")),
        name="mlstm",
    )(za3, zi3, w_conv, b_conv, bif, g_mh, c0, n0, m0, cv0)


def _attn_kernel(cq_ref, ct_ref, st_ref, wn_ref, wr_ref, wrs_ref, wuk_ref, wuv_ref,
                 plat_ref, pkr_ref, klat_ref, kkr_ref, o_ref,
                 ql_ref, qr_ref, s0_ref, s_ref, mrun_ref, mb_ref, lrun_ref, acc_ref,
                 *, tq, tk, n_prefix, n_kt, causal):
    i = pl.program_id(1)
    r = A_HEADS * tq
    tdims = (((1,), (1,)), ((), ()))
    lane_chunks = tk // 128

    cq = cq_ref[...]
    qn_all = jnp.dot(cq, wn_ref[...], preferred_element_type=F32)
    qr_all = jnp.dot(cq, wr_ref[...], preferred_element_type=F32)
    qrs_all = jnp.dot(cq, wrs_ref[...], preferred_element_type=F32)
    ct = ct_ref[...]
    st = st_ref[...]
    for h in range(A_HEADS):
        qn = qn_all[:, h * NOPE_DIM:(h + 1) * NOPE_DIM].astype(BF16)
        ql = jnp.dot(qn, wuk_ref[h], preferred_element_type=F32) * QUERY_SCALE
        ql_ref[h * tq:(h + 1) * tq, :] = ql.astype(BF16)
        qr = (qr_all[:, h * 128:(h + 1) * 128] * ct + qrs_all[:, h * 128:(h + 1) * 128] * st) * QUERY_SCALE
        qr_ref[h * tq:(h + 1) * tq, :] = qr.astype(BF16)

    def scores(kl, kk):
        return (lax.dot_general(ql_ref[...], kl, tdims, preferred_element_type=F32)
                + lax.dot_general(qr_ref[:, 0:ROPE_DIM], kk, tdims, preferred_element_type=F32))

    pl_b = plat_ref[...].astype(BF16)
    s0 = scores(pl_b, pkr_ref[...].astype(BF16))
    pcol = lax.broadcasted_iota(jnp.int32, s0.shape, 1)
    s0 = jnp.where(pcol < n_prefix, s0, -jnp.inf)
    s0_ref[...] = s0
    mrun_ref[...] = s0

    def score_block(j, width, masked):
        start = pl.multiple_of(j * tk, tk)
        kl = klat_ref[pl.ds(start, width), :].astype(BF16)
        kk = kkr_ref[pl.ds(start, width), :].astype(BF16)
        s = scores(kl, kk)
        if masked:
            qpos = i * tq + (lax.broadcasted_iota(jnp.int32, s.shape, 0) & (tq - 1))
            kpos = start + lax.broadcasted_iota(jnp.int32, s.shape, 1)
            shift = CHUNK.bit_length() - 1
            s = jnp.where(jnp.right_shift(kpos, shift) <= jnp.right_shift(qpos, shift), s, -jnp.inf)
        s_ref[j, :, 0:width] = s
        mr = mrun_ref[...]
        for c in range(width // 128):
            mr = jnp.maximum(mr, s[:, c * 128:(c + 1) * 128])
        mrun_ref[...] = mr

    def score_body(j, carry):
        score_block(j, tk, False)
        return carry

    if causal:
        last = (i * tq) // tk
        groups = (i * tq - last * tk + tq + 127) // 128
        lax.fori_loop(0, last, score_body, 0)
        for v in range(1, lane_chunks + 1):
            @pl.when(groups == v)
            def _():
                score_block(last, v * 128, True)
        n_full = last
    else:
        lax.fori_loop(0, n_kt, score_body, 0)
        n_full = n_kt

    mb = jnp.broadcast_to(jnp.max(mrun_ref[...], axis=-1, keepdims=True), (r, 128))
    mb_ref[...] = mb
    p0 = jnp.exp2(s0_ref[...] - mb)
    lrun_ref[...] = p0
    acc_ref[...] = jnp.dot(p0.astype(BF16), pl_b, preferred_element_type=F32)

    def value_block(j, width):
        start = pl.multiple_of(j * tk, tk)
        kl = klat_ref[pl.ds(start, width), :].astype(BF16)
        mbw = jnp.concatenate([mb_ref[...]] * (width // 128), axis=1)
        p = jnp.exp2(s_ref[j, :, 0:width] - mbw)
        lr = lrun_ref[...]
        for c in range(width // 128):
            lr = lr + p[:, c * 128:(c + 1) * 128]
        lrun_ref[...] = lr
        acc_ref[...] += jnp.dot(p.astype(BF16), kl, preferred_element_type=F32)

    def value_body(j, carry):
        value_block(j, tk)
        return carry

    lax.fori_loop(0, n_full, value_body, 0)
    if causal:
        for v in range(1, lane_chunks + 1):
            @pl.when(groups == v)
            def _():
                value_block(last, v * 128)

    o = acc_ref[...] / jnp.sum(lrun_ref[...], axis=-1, keepdims=True)
    for h in range(A_HEADS):
        oh = o[h * tq:(h + 1) * tq, :].astype(BF16)
        o_ref[:, h * V_HEAD:(h + 1) * V_HEAD] = jnp.dot(oh, wuv_ref[h], preferred_element_type=F32).astype(BF16)


def _attention(cqn, seq, ctab, stab, wn, wr, wrs, wuk, wuv, plat, pkr, n_prefix, klat, kkr, causal):
    b = cqn.shape[0] // seq
    tq = min(seq, 128)
    nq = seq // tq
    tkeys = klat.shape[1]
    tk = 512
    n_kt = tkeys // tk
    r = A_HEADS * tq
    cq3 = cqn.reshape(b, seq, Q_LORA)
    psel = (lambda bi: 0) if plat.shape[0] == 1 else (lambda bi: bi)
    kern = functools.partial(_attn_kernel, tq=tq, tk=tk, n_prefix=n_prefix, n_kt=n_kt, causal=causal)
    tab = pl.BlockSpec((tq, 128), lambda bi, qi: (qi, 0))
    out = pl.pallas_call(
        kern,
        out_shape=jax.ShapeDtypeStruct((b, seq, A_HEADS * V_HEAD), BF16),
        grid=(b, nq),
        in_specs=[pl.BlockSpec((None, tq, Q_LORA), lambda bi, qi: (bi, qi, 0)), tab, tab,
                  _resident((Q_LORA, A_HEADS * NOPE_DIM)), _resident((Q_LORA, A_HEADS * 128)),
                  _resident((Q_LORA, A_HEADS * 128)), _resident((A_HEADS, NOPE_DIM, KV_LORA)),
                  _resident((A_HEADS, KV_LORA, V_HEAD)),
                  pl.BlockSpec((None, PREFIX_PAD, KV_LORA), lambda bi, qi: (psel(bi), 0, 0)),
                  pl.BlockSpec((None, PREFIX_PAD, ROPE_DIM), lambda bi, qi: (psel(bi), 0, 0)),
                  pl.BlockSpec((None, tkeys, KV_LORA), lambda bi, qi: (bi, 0, 0)),
                  pl.BlockSpec((None, tkeys, ROPE_DIM), lambda bi, qi: (bi, 0, 0))],
        out_specs=pl.BlockSpec((None, tq, A_HEADS * V_HEAD), lambda bi, qi: (bi, qi, 0)),
        scratch_shapes=[pltpu.VMEM((r, KV_LORA), BF16), pltpu.VMEM((r, 128), BF16),
                        pltpu.VMEM((r, PREFIX_PAD), F32), pltpu.VMEM((n_kt, r, tk), F32),
                        pltpu.VMEM((r, 128), F32), pltpu.VMEM((r, 128), F32), pltpu.VMEM((r, 128), F32),
                        pltpu.VMEM((r, KV_LORA), F32)],
        compiler_params=_cparams(("parallel", "arbitrary")),
        name="attention",
    )(cq3, ctab, stab, wn, wr, wrs, wuk, wuv, plat, pkr, klat, kkr)
    return out.reshape(b * seq, A_HEADS * V_HEAD)


ROW_TILE = (D_MODEL // 128, 128)


def _store_row_tiles(ref, rows):
    n = rows.shape[0]
    for s in range(ROW_TILE[0]):
        ref[pl.ds(s, n, stride=ROW_TILE[0]), :] = rows[:, s * 128:(s + 1) * 128]


def _load_row_tiles(ref, n):
    return [ref[pl.ds(s, n, stride=ROW_TILE[0]), :] for s in range(ROW_TILE[0])]


def _merge_kernel(hg_ref, ov_ref, gab_ref, x_ref, wa_ref, wb_ref, wo_ref, gf_ref, wrh_ref, wrl_ref, br_ref,
                  x1_ref, x2_ref, ti_ref, tg_ref, tr_ref, cnt_ref, carry_ref):
    i = pl.program_id(0)

    @pl.when(i == 0)
    def _():
        carry_ref[...] = jnp.zeros_like(carry_ref)

    tm = x_ref.shape[0]
    ya = jnp.dot(hg_ref[...], wa_ref[...], preferred_element_type=F32)
    yb = jnp.dot(ov_ref[...], wb_ref[...], preferred_element_type=F32)
    mixed = (_sigmoid(gab_ref[:, 0:D_MODEL].astype(F32)) * ya
             + _sigmoid(gab_ref[:, D_MODEL:2 * D_MODEL].astype(F32)) * yb)
    x1 = x_ref[...] + jnp.dot(mixed.astype(BF16), wo_ref[...], preferred_element_type=F32)
    x1_ref[...] = x1
    x2 = x1 * lax.rsqrt(jnp.mean(x1 * x1, axis=-1, keepdims=True) + EPS) * gf_ref[...]
    _store_row_tiles(x2_ref, x2)

    lane = lax.broadcasted_iota(jnp.int32, (tm, 128), 1)
    lane_f = lane.astype(F32)
    x2_hi = x2.astype(BF16)
    x2_lo = (x2 - x2_hi.astype(F32)).astype(BF16)
    logits = (jnp.dot(x2_hi, wrh_ref[...], preferred_element_type=F32)
              + jnp.dot(x2_lo, wrh_ref[...], preferred_element_type=F32)
              + jnp.dot(x2_hi, wrl_ref[...], preferred_element_type=F32) + br_ref[...])
    cur = jnp.where(lane < N_EXPERTS, logits, -jnp.inf)
    vals, idxs = [], []
    for _ in range(TOP_K):
        mx = jnp.max(cur, axis=-1, keepdims=True)
        idx = jnp.min(jnp.where(cur == mx, lane_f, 128.0), axis=-1, keepdims=True)
        vals.append(mx)
        idxs.append(idx)
        cur = jnp.where(lane_f == idx, -jnp.inf, cur)
    es = [jnp.exp(v - vals[0]) for v in vals]
    tot = es[0] + es[1] + es[2] + es[3]

    onehots = [(lane_f == idx) for idx in idxs]
    cnt = jnp.zeros((tm, 128), F32)
    for oh in onehots:
        cnt = cnt + jnp.where(oh, 1.0, 0.0)
    rr = lax.broadcasted_iota(jnp.int32, (tm, tm), 0)
    cc = lax.broadcasted_iota(jnp.int32, (tm, tm), 1)
    before = jnp.dot((rr > cc).astype(BF16), cnt.astype(BF16), preferred_element_type=F32) + carry_ref[0:1, :]
    ti = jnp.zeros((tm, 128), F32)
    tg = jnp.zeros((tm, 128), F32)
    tr = jnp.zeros((tm, 128), F32)
    for k in range(TOP_K):
        rank = jnp.sum(jnp.where(onehots[k], before, 0.0), axis=-1, keepdims=True)
        ti = jnp.where(lane == k, idxs[k], ti)
        tg = jnp.where(lane == k, es[k] / tot, tg)
        tr = jnp.where(lane == k, rank, tr)
    ti_ref[...] = ti.astype(jnp.int32)
    tg_ref[...] = tg
    tr_ref[...] = tr.astype(jnp.int32)
    new_carry = carry_ref[0:1, :] + jnp.sum(cnt, axis=0, keepdims=True)
    carry_ref[...] = jnp.broadcast_to(new_carry, carry_ref.shape)
    cnt_ref[...] = jnp.broadcast_to(new_carry, cnt_ref.shape).astype(jnp.int32)


def _merge_route(hg, ov, gab, x2d, wa, wb, wo, g_ffn, w_router_hi, w_router_lo, b_router):
    m = x2d.shape[0]
    tm = min(512, m)
    row = lambda w: pl.BlockSpec((tm, w), lambda i: (i, 0))
    sq = _resident((D_MODEL, D_MODEL))
    return pl.pallas_call(
        _merge_kernel,
        out_shape=(jax.ShapeDtypeStruct((m, D_MODEL), F32), jax.ShapeDtypeStruct((m * ROW_TILE[0], 128), F32),
                   jax.ShapeDtypeStruct((m, 128), jnp.int32), jax.ShapeDtypeStruct((m, 128), F32),
                   jax.ShapeDtypeStruct((m, 128), jnp.int32), jax.ShapeDtypeStruct((8, 128), jnp.int32)),
        grid=(m // tm,),
        in_specs=[row(M_V), row(A_HEADS * V_HEAD), row(W_G), row(D_MODEL), sq, sq, sq,
                  _resident((1, D_MODEL)), _resident((D_MODEL, 128)), _resident((D_MODEL, 128)),
                  _resident((1, 128))],
        out_specs=(row(D_MODEL), pl.BlockSpec((tm * ROW_TILE[0], 128), lambda i: (i, 0)), row(128), row(128),
                   row(128), pl.BlockSpec((8, 128), lambda i: (0, 0))),
        scratch_shapes=[pltpu.VMEM((8, 128), F32)],
        compiler_params=_cparams(("arbitrary",)),
        name="merge_route",
    )(hg, ov, gab, x2d, wa, wb, wo, g_ffn, w_router_hi, w_router_lo, b_router)


def _dispatch_kernel(zt_ref, dest_ref, x_ref, *refs):
    xs_ref, zbuf_ref, sem, zsem = refs[-4:]
    i = pl.program_id(0)
    tr = x_ref.shape[0]
    tm = zbuf_ref.shape[0]

    @pl.when((i == 0) & (zt_ref[0] > 0))
    def _():
        zbuf_ref[...] = jnp.zeros_like(zbuf_ref)

        def start_zero(j, carry):
            start = pl.multiple_of(zt_ref[1 + j], tm)
            pltpu.make_async_copy(zbuf_ref, xs_ref.at[pl.ds(start, tm)], zsem).start()
            return carry

        def wait_zero(j, carry):
            pltpu.make_async_copy(zbuf_ref, xs_ref.at[pl.ds(0, tm)], zsem).wait()
            return carry

        lax.fori_loop(0, zt_ref[0], start_zero, 0)
        lax.fori_loop(0, zt_ref[0], wait_zero, 0)

    def issue(r, carry):
        for k in range(TOP_K):
            pltpu.make_async_copy(x_ref.at[r], xs_ref.at[dest_ref[0, 0, r * TOP_K + k]],
                                  sem).start(priority=k % 2)
        return carry

    lax.fori_loop(0, tr, issue, 0, unroll=8)
    for _ in range(TOP_K):
        pltpu.make_async_copy(x_ref, xs_ref.at[pl.ds(0, tr)], sem).wait()


def _dispatch(x2t, dest, zero_tiles, n_rows, xs_prev=None):
    m = x2t.shape[0] // ROW_TILE[0]
    tr = min(256, m)
    nt = m // tr
    dest3 = dest.reshape(nt, 1, tr * TOP_K)
    in_specs = [pl.BlockSpec((1, 1, tr * TOP_K), lambda i, zt: (i, 0, 0), memory_space=pltpu.SMEM),
                pl.BlockSpec((tr,) + ROW_TILE, lambda i, zt: (i, 0, 0))]
    args = [zero_tiles, dest3, x2t.reshape((m,) + ROW_TILE)]
    aliases = {}
    if xs_prev is not None:
        in_specs.append(pl.BlockSpec(memory_space=pl.ANY))
        args.append(xs_prev)
        aliases = {3: 0}
    grid_spec = pltpu.PrefetchScalarGridSpec(
        num_scalar_prefetch=1,
        grid=(nt,),
        in_specs=in_specs,
        out_specs=pl.BlockSpec(memory_space=pl.ANY),
        scratch_shapes=[pltpu.VMEM((MOE_TILE,) + ROW_TILE, F32), pltpu.SemaphoreType.DMA(()),
                        pltpu.SemaphoreType.DMA(())],
    )
    return pl.pallas_call(
        _dispatch_kernel,
        out_shape=jax.ShapeDtypeStruct((n_rows,) + ROW_TILE, F32),
        grid_spec=grid_spec,
        input_output_aliases=aliases,
        compiler_params=_cparams(("arbitrary",)),
        name="moe_dispatch",
    )(*args)


def _expert_kernel(te_ref, nu_ref, xs_ref, wgu_ref, bgu_ref, wdn_ref, bdn_ref, ys_ref, wgu_b, wdn_b):
    i = pl.program_id(0)
    prev = te_ref[jnp.maximum(i - 1, 0)]

    @pl.when(i < nu_ref[0])
    def _():
        @pl.when((i == 0) | (te_ref[i] != prev))
        def _():
            rc = 128
            for c in range(D_MODEL // rc):
                wgu_b[c * rc:(c + 1) * rc, :] = wgu_ref[c * rc:(c + 1) * rc, :].astype(BF16)
                wdn_b[c * rc:(c + 1) * rc, :] = wdn_ref[c * rc:(c + 1) * rc, :].astype(BF16)

        tm = xs_ref.shape[0] // ROW_TILE[0]
        xb = jnp.concatenate([c.astype(BF16) for c in _load_row_tiles(xs_ref, tm)], axis=1)
        fc = 512
        y = jnp.zeros((tm, D_MODEL), F32) + bdn_ref[...]
        for f0 in range(0, D_FF, fc):
            def proj(c0):
                return jnp.dot(xb, wgu_b[:, c0:c0 + fc], preferred_element_type=F32) + bgu_ref[:, c0:c0 + fc]
            gate = jnp.minimum(proj(f0), SWIGLU_LIMIT)
            up = jnp.clip(proj(D_FF + f0), -SWIGLU_LIMIT, SWIGLU_LIMIT)
            hid = (up + 1.0) * gate * _sigmoid(SWIGLU_ALPHA * gate)
            y = y + jnp.dot(hid.astype(BF16), wdn_b[f0:f0 + fc, :], preferred_element_type=F32)
        _store_row_tiles(ys_ref, y)

    @pl.when(i >= nu_ref[0])
    def _():
        ys_ref[...] = jnp.zeros_like(ys_ref)


def _experts(xs, tile_expert, n_used, w_gu, b_gu, w_dn, b_dn):
    n_rows, w = xs.shape
    tm = MOE_TILE * ROW_TILE[0]
    nt = n_rows // tm
    tile = lambda i, te, nu: (jnp.minimum(i, jnp.maximum(nu[0] - 1, 0)), 0)
    grid_spec = pltpu.PrefetchScalarGridSpec(
        num_scalar_prefetch=2,
        grid=(nt,),
        in_specs=[pl.BlockSpec((tm, w), tile),
                  pl.BlockSpec((None, D_MODEL, 2 * D_FF), lambda i, te, nu: (te[i], 0, 0)),
                  pl.BlockSpec((None, 1, 2 * D_FF), lambda i, te, nu: (te[i], 0, 0)),
                  pl.BlockSpec((None, D_FF, D_MODEL), lambda i, te, nu: (te[i], 0, 0)),
                  pl.BlockSpec((None, 1, D_MODEL), lambda i, te, nu: (te[i], 0, 0))],
        out_specs=pl.BlockSpec((tm, w), lambda i, te, nu: (i, 0)),
        scratch_shapes=[pltpu.VMEM((D_MODEL, 2 * D_FF), BF16), pltpu.VMEM((D_FF, D_MODEL), BF16)],
    )
    return pl.pallas_call(
        _expert_kernel,
        out_shape=jax.ShapeDtypeStruct((n_rows, w), F32),
        grid_spec=grid_spec,
        compiler_params=_cparams(("arbitrary",)),
        name="moe_experts",
    )(tile_expert, n_used, xs, w_gu, b_gu, w_dn, b_dn)


def _combine_kernel(dcur_ref, dnext_ref, x1_ref, tg_ref, gfin_ref, ys_ref, y_ref, buf_ref, sem_ref, *, nt):
    i = pl.program_id(0)
    tc = x1_ref.shape[0]

    def issue(dref, slot):
        def body(r, carry):
            for k in range(TOP_K):
                src0 = pl.multiple_of(dref[0, 0, r * TOP_K + k] * ROW_TILE[0], ROW_TILE[0])
                dst0 = pl.multiple_of(r * ROW_TILE[0], ROW_TILE[0])
                pltpu.make_async_copy(ys_ref.at[pl.ds(src0, ROW_TILE[0]), :],
                                      buf_ref.at[slot, k, pl.ds(dst0, ROW_TILE[0]), :],
                                      sem_ref.at[slot]).start(priority=k % 2)
            return carry
        lax.fori_loop(0, tc, body, 0, unroll=8)

    slot = i % 2

    @pl.when(i == 0)
    def _():
        issue(dcur_ref, 0)

    @pl.when(i + 1 < nt)
    def _():
        issue(dnext_ref, 1 - slot)

    for k in range(TOP_K):
        pltpu.make_async_copy(ys_ref.at[pl.ds(0, tc * ROW_TILE[0]), :], buf_ref.at[slot, k],
                              sem_ref.at[slot]).wait()

    gates = [tg_ref[:, k:k + 1] for k in range(TOP_K)]
    chunks = []
    ssq = jnp.zeros((tc, 1), F32)
    for s in range(ROW_TILE[0]):
        xc = x1_ref[:, s * 128:(s + 1) * 128]
        for k in range(TOP_K):
            xc = xc + gates[k] * buf_ref[slot, k, pl.ds(s, tc, stride=ROW_TILE[0]), :]
        chunks.append(xc)
        ssq = ssq + jnp.sum(xc * xc, axis=-1, keepdims=True)
    inv = lax.rsqrt(ssq / D_MODEL + EPS)
    for s in range(ROW_TILE[0]):
        y_ref[:, s * 128:(s + 1) * 128] = chunks[s] * inv * gfin_ref[:, s * 128:(s + 1) * 128]


def _combine(dest, x1, tg, g_final, ys):
    m = x1.shape[0]
    tc = min(256, m)
    nt = m // tc
    dest3 = dest.reshape(nt, 1, tc * TOP_K)
    kern = functools.partial(_combine_kernel, nt=nt)
    smem = lambda f: pl.BlockSpec((1, 1, tc * TOP_K), f, memory_space=pltpu.SMEM)
    return pl.pallas_call(
        kern,
        out_shape=jax.ShapeDtypeStruct((m, D_MODEL), F32),
        grid=(nt,),
        in_specs=[smem(lambda i: (i, 0, 0)), smem(lambda i: (jnp.minimum(i + 1, nt - 1), 0, 0)),
                  pl.BlockSpec((tc, D_MODEL), lambda i: (i, 0)), pl.BlockSpec((tc, 128), lambda i: (i, 0)),
                  _resident((1, D_MODEL)), pl.BlockSpec(memory_space=pl.ANY)],
        out_specs=pl.BlockSpec((tc, D_MODEL), lambda i: (i, 0)),
        scratch_shapes=[pltpu.VMEM((2, TOP_K, tc * ROW_TILE[0], 128), F32), pltpu.SemaphoreType.DMA((2,))],
        compiler_params=_cparams(("arbitrary",)),
        name="moe_combine",
    )(dest3, dest3, x1, tg, g_final, ys)


def _rope_tables(pos):
    half = ROPE_DIM // 2
    freqs = ROPE_THETA ** (-jnp.arange(half, dtype=F32) / half)
    ang = pos.astype(F32)[:, None] * freqs[None, :]
    cos, sin = jnp.cos(ang), jnp.sin(ang)
    return jnp.concatenate([cos, cos], axis=-1), jnp.concatenate([-sin, sin], axis=-1)


def _tile_rows(t, rows):
    return t if t.shape[0] >= rows else jnp.tile(t, (rows // t.shape[0], 1))


def _pad_lanes(t, width):
    return jnp.pad(t, ((0, 0), (0, width - t.shape[1])))


def _moe(streams, g_final, w_gu, b_gu, w_dn, b_dn):
    tm = MOE_TILE
    a = sum(s[0].shape[0] for s in streams) * TOP_K
    n_tiles = -(-a // tm) + N_EXPERTS
    counts = [s[5][0, :N_EXPERTS] for s in streams]
    total = sum(counts)
    pcounts = (total + tm - 1) // tm * tm
    pend = jnp.cumsum(pcounts)
    n_used = (pend[-1] // tm).astype(jnp.int32)
    tiles = jnp.minimum(jnp.arange(n_tiles, dtype=jnp.int32), n_used - 1) * tm
    tile_expert = jnp.minimum(jnp.sum(pend[None, :] <= tiles[:, None], axis=-1), N_EXPERTS - 1).astype(jnp.int32)
    eids = jnp.arange(N_EXPERTS, dtype=jnp.int32)
    base = pend - pcounts
    rest_tiles = (a - streams[0][0].shape[0] * TOP_K) // tm
    first = (base + counts[0]) // tm
    cand = first[:, None] + jnp.arange(rest_tiles + 2, dtype=jnp.int32)[None, :]
    cand = jnp.where((cand < (pend // tm)[:, None]) & (pcounts[:, None] > 0), cand, -1)
    tail = n_used + jnp.arange(N_EXPERTS, dtype=jnp.int32)
    cand = jnp.concatenate([cand.reshape(-1), jnp.where(tail < n_tiles, tail, -1)])
    n_zero = jnp.sum(cand >= 0).astype(jnp.int32)
    zlist = (-jnp.sort(-cand))[:rest_tiles + 3 * N_EXPERTS] * tm
    zero_tiles = jnp.concatenate([n_zero[None], zlist]).astype(jnp.int32)
    dests, xs = [], None
    for s, (_, x2t, ti, _, tr, _) in enumerate(streams):
        dest = tr[:, :TOP_K] + jnp.sum(jnp.where(ti[:, :TOP_K, None] == eids, base, 0), axis=-1)
        dests.append(dest.astype(jnp.int32))
        zt = zero_tiles if s == 0 else jnp.zeros_like(zero_tiles)
        xs = _dispatch(x2t, dests[-1], zt, n_tiles * tm, xs)
        base = base + counts[s]
    ys = _experts(xs.reshape(n_tiles * tm * ROW_TILE[0], 128), tile_expert, n_used.reshape(1),
                  w_gu, b_gu, w_dn, b_dn)
    return [_combine(dest, s[0], s[3], g_final, ys) for dest, s in zip(dests, streams)]


def kernel(x_prompt, x_sample, cache_kv_latent, cache_k_rope, state_mlstm_C, state_mlstm_n, state_mlstm_m, state_conv, meta_tokens, g_mix_norm, w_in, b_if, w_conv, b_conv, g_mh_norm, w_proj_a, g_q_norm, g_kv_norm, w_uq, w_uk, w_uv, w_proj_b, w_out, g_ffn_norm, w_router, b_router, w_gate_up, b_gate_up, w_down, b_down, g_final_norm):
    bsz, seq = x_prompt.shape[0], x_prompt.shape[1]
    dbs, dseq = x_sample.shape[0], x_sample.shape[1]
    past = cache_kv_latent.shape[2]
    assert w_in.shape[0] == 1, "single-layer trunk"
    l = 0

    wi = w_in[l]
    o_gate = 2 * M_QK + 2 * M_V
    o_cq = o_gate + 2 * M_HEADS
    o_ckv = o_cq + Q_LORA
    o_kr = o_ckv + KV_LORA
    o_g = o_kr + ROPE_DIM
    swap = np.concatenate([np.arange(ROPE_DIM // 2, ROPE_DIM), np.arange(ROPE_DIM // 2)])
    w_kr = wi[:, o_kr:o_g]
    w_cat = jnp.concatenate([wi[:, :o_gate], _pad_lanes(wi[:, o_gate:o_cq], W_GATE), wi[:, o_cq:o_kr], w_kr,
                             w_kr[:, swap], wi[:, o_g:]], axis=1).astype(BF16)
    bif = _pad_lanes(b_if[l][None, :], W_GATE)
    uq = w_uq[l].reshape(Q_LORA, A_HEADS, NOPE_DIM + ROPE_DIM)
    wn = uq[:, :, :NOPE_DIM].reshape(Q_LORA, A_HEADS * NOPE_DIM).astype(BF16)
    uq_r = uq[:, :, NOPE_DIM:]
    pad_r = lambda t: jnp.pad(t, ((0, 0), (0, 0), (0, 128 - ROPE_DIM))).reshape(Q_LORA, A_HEADS * 128).astype(BF16)
    wr, wrs = pad_r(uq_r), pad_r(uq_r[:, :, swap])
    wuk = jnp.transpose(w_uk[l], (1, 2, 0)).astype(BF16)
    wuv = jnp.transpose(w_uv[l], (1, 0, 2)).astype(BF16)
    wa, wb, wo = w_proj_a[l].astype(BF16), w_proj_b[l].astype(BF16), w_out[l].astype(BF16)
    w_rt = _pad_lanes(w_router[l], 128)
    w_rt_hi = w_rt.astype(BF16)
    w_rt_lo = (w_rt - w_rt_hi.astype(F32)).astype(BF16)
    b_rt = _pad_lanes(b_router[l][None, :], 128)
    g_mix, g_q, g_kv = g_mix_norm[l][None, :], g_q_norm[l][None, :], g_kv_norm[l][None, :]
    g_mh, g_ffn, g_fin = g_mh_norm[l][None, :], g_ffn_norm[l][None, :], g_final_norm[None, :]
    wcv, bcv = w_conv[l], b_conv[l][None, :]
    w_gu, b_gu = w_gate_up[l], b_gate_up[l][:, None, :]
    w_dn, b_dn = w_down[l], b_down[l][:, None, :]

    ct_m, st_m = _rope_tables(jnp.arange(N_META))
    ct_p, st_p = _rope_tables(N_META + jnp.arange(seq))
    ct_s, st_s = _rope_tables(N_META + past + jnp.arange(dseq))

    def stream_tables(ct, st, m_rows):
        rows = max(ct.shape[0], min(512, m_rows))
        return _tile_rows(ct, rows), _tile_rows(st, rows)

    def prefix(lat, kr):
        pad = ((0, 0), (0, PREFIX_PAD - lat.shape[1]), (0, 0))
        return jnp.pad(lat, pad), jnp.pad(kr, pad)

    za_m, zi_m, _, lat_m, kr_m, _, tail_m = _inproj(meta_tokens, N_META, g_mix, w_cat, g_q, g_kv, ct_m, st_m)
    npad = CHUNK - N_META
    za_mp = jnp.concatenate([jnp.zeros((npad, W_A), BF16), za_m], axis=0)
    lane = jnp.arange(W_GATE)
    neutral = jnp.where(lane < M_HEADS, -1e30, jnp.where(lane < 2 * M_HEADS, 1e30, 0.0)).astype(F32)
    zi_mp = jnp.concatenate([jnp.broadcast_to(neutral, (npad, W_GATE)), zi_m], axis=0)
    zeros_state = (jnp.zeros((1, M_HEADS, M_DK, M_DV), F32), jnp.zeros((1, M_HEADS, 1, M_DK), F32),
                   jnp.zeros((1, M_HEADS, 1, 128), F32), jnp.zeros((1, 8, 2 * M_QK), F32))
    _, c_m, n_m, m_m = _mlstm(za_mp, zi_mp, CHUNK, wcv, bcv, bif, g_mh, *zeros_state)

    def route(x2d, hg, ov, gab):
        return _merge_route(hg.reshape(x2d.shape[0], M_V), ov, gab, x2d, wa, wb, wo, g_ffn, w_rt_hi, w_rt_lo, b_rt)

    xp = x_prompt.reshape(bsz * seq, D_MODEL)
    ctp, stp = stream_tables(ct_p, st_p, bsz * seq)
    za, zi, cqn, lat, kr, gab, tail = _inproj(xp, seq, g_mix, w_cat, g_q, g_kv, ctp, stp)
    hg, c_p, n_p, m_p = _mlstm(za, zi, seq, wcv, bcv, bif, g_mh, c_m, n_m, m_m, tail_m)
    plat, pkr = prefix(lat_m[None], kr_m[None])
    ctq, stq = _pad_lanes(ct_p, 128), _pad_lanes(st_p, 128)
    ov = _attention(cqn, seq, ctq, stq, wn, wr, wrs, wuk, wuv, plat, pkr, N_META,
                    lat.reshape(bsz, seq, KV_LORA), kr.reshape(bsz, seq, ROPE_DIM), True)
    routed_p = route(xp, hg, ov, gab)

    xs2 = x_sample.reshape(dbs * dseq, D_MODEL)
    cts, sts = stream_tables(ct_s, st_s, dbs * dseq)
    za, zi, cqn, lat_s, kr_s, gab, tail_s = _inproj(xs2, dseq, g_mix, w_cat, g_q, g_kv, cts, sts)
    cv0 = jnp.pad(state_conv[l], ((0, 0), (8 - (CONV_W - 1), 0), (0, 0)))
    m0 = jnp.broadcast_to(state_mlstm_m[l][:, :, None, None], (dbs, M_HEADS, 1, 128))
    hg, c_s, n_s, m_s = _mlstm(za, zi, dseq, wcv, bcv, bif, g_mh, state_mlstm_C[l],
                               state_mlstm_n[l][:, :, None, :], m0, cv0)
    plat, pkr = prefix(lat_s.reshape(dbs, dseq, KV_LORA), kr_s.reshape(dbs, dseq, ROPE_DIM))
    ov = _attention(cqn, dseq, _pad_lanes(ct_s, 128), _pad_lanes(st_s, 128), wn, wr, wrs, wuk, wuv, plat, pkr,
                    dseq, cache_kv_latent[l], cache_k_rope[l], False)
    routed_s = route(xs2, hg, ov, gab)

    y_p, y_s = _moe([routed_p, routed_s], g_fin, w_gu, b_gu, w_dn, b_dn)
    y_prompt = y_p.reshape(bsz, seq, D_MODEL)
    y_sample = y_s.reshape(dbs, dseq, D_MODEL)

    def with_meta(meta_rows, frames, width):
        return jnp.concatenate([jnp.broadcast_to(meta_rows[None], (bsz, N_META, width)),
                                frames.reshape(bsz, seq, width)], axis=1)[None]

    p_lat = with_meta(lat_m, lat, KV_LORA)
    p_kr = with_meta(kr_m, kr, ROPE_DIM)
    tail3 = lambda t: t[:, 8 - (CONV_W - 1):, :][None]
    return (y_prompt, y_sample, p_lat, p_kr, c_p[None], n_p[:, :, 0, :][None], m_p[:, :, 0, 0][None], tail3(tail),
            lat_s.reshape(dbs, dseq, KV_LORA)[None], kr_s.reshape(dbs, dseq, ROPE_DIM)[None], c_s[None],
            n_s[:, :, 0, :][None], m_s[:, :, 0, 0][None], tail3(tail_s))
```

```python
import functools

import jax
import jax.numpy as jnp
import numpy as np
from jax import lax
from jax.experimental import pallas as pl
from jax.experimental.pallas import tpu as pltpu

F32 = jnp.float32
BF16 = jnp.bfloat16
HIGHEST = lax.Precision.HIGHEST

D_MODEL = 1024
N_META = 16
CHUNK = 64
M_HEADS = 4
M_DK = 256
M_DV = 256
M_QK = M_HEADS * M_DK
M_V = M_HEADS * M_DV
CONV_W = 4
A_HEADS = 8
NOPE_DIM = 128
ROPE_DIM = 64
V_HEAD = 128
Q_LORA = 384
KV_LORA = 256
ROPE_THETA = 10000.0
ATTN_SCALE = (NOPE_DIM + ROPE_DIM) ** -0.5
QUERY_SCALE = ATTN_SCALE * float(np.log2(np.e))
N_EXPERTS = 32
TOP_K = 4
D_FF = 1024
SWIGLU_LIMIT = 7.0
SWIGLU_ALPHA = 1.702
EPS = 1e-6

COL_A = 0
W_A = 2 * M_QK + 2 * M_V
COL_GATE = COL_A + W_A
W_GATE = 128
COL_B = COL_GATE + W_GATE
W_B = Q_LORA + KV_LORA + 2 * ROPE_DIM
COL_G = COL_B + W_B
W_G = 2 * D_MODEL
W_IN_COLS = COL_G + W_G

V7X_VMEM_BYTES = 64 * 2**20
VMEM_LIMIT = V7X_VMEM_BYTES - 8 * 2**20

PREFIX_PAD = 128
MOE_TILE = 512


def _sigmoid(x):
    return 1.0 / (1.0 + jnp.exp(-x))


def _cparams(sem):
    return pltpu.CompilerParams(dimension_semantics=sem, vmem_limit_bytes=VMEM_LIMIT)


def _resident(shape):
    nd = len(shape)
    return pl.BlockSpec(shape, lambda *_: (0,) * nd, pipeline_mode=pl.Buffered(1))


def _inproj_kernel(x_ref, g_ref, w_ref, gq_ref, gkv_ref, ct_ref, st_ref,
                   za_ref, zi_ref, cq_ref, lat_ref, kr_ref, gab_ref, tail_ref, xn_ref, *, seq, spt):
    x = x_ref[...]
    xn = x * lax.rsqrt(jnp.mean(x * x, axis=-1, keepdims=True) + EPS) * g_ref[...]
    xn_ref[...] = xn.astype(BF16)
    xb = xn_ref[...]
    rows = x.shape[0]
    for c in range(W_A // 1024):
        acc = jnp.dot(xb, w_ref[:, COL_A + c * 1024:COL_A + (c + 1) * 1024], preferred_element_type=F32)
        za_ref[:, c * 1024:(c + 1) * 1024] = acc.astype(BF16)
        if c * 1024 < 2 * M_QK:
            if spt == 1:
                tail_ref[0, :, c * 1024:(c + 1) * 1024] = acc[rows - 8:rows, :]
            else:
                for s in range(spt):
                    tail_ref[s, :, c * 1024:(c + 1) * 1024] = acc[(s + 1) * seq - 8:(s + 1) * seq, :]
    zi_ref[...] = jnp.dot(xb, w_ref[:, COL_GATE:COL_GATE + W_GATE], preferred_element_type=F32)
    zb = jnp.dot(xb, w_ref[:, COL_B:COL_B + W_B], preferred_element_type=F32)
    cq = zb[:, 0:Q_LORA]
    cq_ref[...] = (cq * lax.rsqrt(jnp.mean(cq * cq, axis=-1, keepdims=True) + EPS) * gq_ref[...]).astype(BF16)
    ckv = zb[:, Q_LORA:Q_LORA + KV_LORA]
    lat_ref[...] = ckv * lax.rsqrt(jnp.mean(ckv * ckv, axis=-1, keepdims=True) + EPS) * gkv_ref[...]
    k_r = zb[:, Q_LORA + KV_LORA:Q_LORA + KV_LORA + ROPE_DIM]
    k_rs = zb[:, Q_LORA + KV_LORA + ROPE_DIM:W_B]
    kr_ref[...] = k_r * ct_ref[...] + k_rs * st_ref[...]
    for c in range(W_G // 1024):
        acc = jnp.dot(xb, w_ref[:, COL_G + c * 1024:COL_G + (c + 1) * 1024], preferred_element_type=F32)
        gab_ref[:, c * 1024:(c + 1) * 1024] = acc.astype(BF16)


def _inproj(x2d, seq, g_mix, w_cat, g_q, g_kv, ctab, stab):
    m = x2d.shape[0]
    nseq = m // seq
    tm = min(512, m)
    spt = max(1, tm // seq)
    tps = max(1, seq // tm)
    nt = m // tm
    kern = functools.partial(_inproj_kernel, seq=seq, spt=spt)
    row = lambda w: pl.BlockSpec((tm, w), lambda i: (i, 0))
    tab = pl.BlockSpec((tm, ROPE_DIM), lambda i: (i % tps, 0))
    if spt == 1:
        tail_spec = pl.BlockSpec((1, 8, 2 * M_QK), lambda i: (i // tps, 0, 0))
    else:
        tail_spec = pl.BlockSpec((spt, 8, 2 * M_QK), lambda i: (i, 0, 0))
    return pl.pallas_call(
        kern,
        out_shape=(jax.ShapeDtypeStruct((m, W_A), BF16), jax.ShapeDtypeStruct((m, W_GATE), F32),
                   jax.ShapeDtypeStruct((m, Q_LORA), BF16), jax.ShapeDtypeStruct((m, KV_LORA), F32),
                   jax.ShapeDtypeStruct((m, ROPE_DIM), F32), jax.ShapeDtypeStruct((m, W_G), BF16),
                   jax.ShapeDtypeStruct((nseq, 8, 2 * M_QK), F32)),
        grid=(nt,),
        in_specs=[row(D_MODEL), _resident((1, D_MODEL)), _resident((D_MODEL, W_IN_COLS)),
                  _resident((1, Q_LORA)), _resident((1, KV_LORA)), tab, tab],
        out_specs=(row(W_A), row(W_GATE), row(Q_LORA), row(KV_LORA), row(ROPE_DIM), row(W_G), tail_spec),
        scratch_shapes=[pltpu.VMEM((tm, D_MODEL), BF16)],
        compiler_params=_cparams(("arbitrary",)),
        name="inproj",
    )(x2d, g_mix, w_cat, g_q, g_kv, ctab, stab)


def _mlstm_kernel(za_ref, zi_ref, wc_ref, bc_ref, bif_ref, gmh_ref, c0_ref, n0_ref, m0_ref, cv0_ref,
                  hg_ref, cout_ref, nout_ref, mout_ref, cs_ref, ns_ref, ms_ref, cbuf_ref, *, L, nc):
    c = pl.program_id(1)

    @pl.when(c == 0)
    def _():
        cs_ref[...] = c0_ref[...]
        ns_ref[...] = n0_ref[...]
        ms_ref[...] = m0_ref[...]
        cbuf_ref[...] = cv0_ref[...]

    g = zi_ref[...] + bif_ref[...]
    lf = jnp.minimum(g, 0.0) - jnp.log(1.0 + jnp.exp(-jnp.abs(g)))
    row = lax.broadcasted_iota(jnp.int32, (L, L), 0)
    col = lax.broadcasted_iota(jnp.int32, (L, L), 1)
    causal = row >= col
    tdims = (((1,), (1,)), ((), ()))

    def dot01(a01, x, dims):
        a = a01.astype(BF16)
        x_hi = x.astype(BF16)
        x_lo = (x - x_hi.astype(F32)).astype(BF16)
        return (lax.dot_general(a, x_hi, dims, preferred_element_type=F32)
                + lax.dot_general(a, x_lo, dims, preferred_element_type=F32))

    bcum = dot01(causal, lf, (((1,), (0,)), ((), ())))
    sub8 = lax.broadcasted_iota(jnp.int32, (8, 128), 0)
    lane8 = lax.broadcasted_iota(jnp.int32, (8, 128), 1)
    i_rows = dot01(lane8 == sub8, g, tdims)
    b_rows = dot01(lane8 == sub8 + M_HEADS, bcum, tdims)

    delays = [(row - col == d).astype(BF16) for d in range(1, CONV_W)]
    row8 = lax.broadcasted_iota(jnp.int32, (8, M_DK), 0)

    def conv_silu(col0):
        cols = slice(col0, col0 + M_DK)
        ub = za_ref[:, cols]
        u = ub.astype(F32)
        prev = cbuf_ref[:, cols]
        acc = bc_ref[:, cols] + u * wc_ref[CONV_W - 1:CONV_W, cols]
        for d in range(1, CONV_W):
            sh = jnp.dot(delays[d - 1], ub, preferred_element_type=F32)
            head = sh[0:8, :] + jnp.where(row8 < d, pltpu.roll(prev, d, axis=0), 0.0)
            sh = jnp.concatenate([head, sh[8:, :]], axis=0)
            acc = acc + sh * wc_ref[CONV_W - 1 - d:CONV_W - d, cols]
        cbuf_ref[:, cols] = u[L - 8:L, :]
        return acc * _sigmoid(acc)

    for h in range(M_HEADS):
        q = conv_silu(h * M_DK)
        k = conv_silu(M_QK + h * M_DK) * (M_DK ** -0.5)
        v = za_ref[:, 2 * M_QK + h * M_DV:2 * M_QK + (h + 1) * M_DV]
        o = za_ref[:, 2 * M_QK + M_V + h * M_DV:2 * M_QK + M_V + (h + 1) * M_DV].astype(F32)
        i_row = i_rows[h:h + 1, :]
        b_row = b_rows[h:h + 1, :]
        i_col = g[:, h:h + 1]
        b_col = bcum[:, M_HEADS + h:M_HEADS + h + 1]
        m_prev = ms_ref[h][:, 0:1]
        dmat = jnp.where(causal, b_col - b_row + i_row, -jnp.inf)
        inter = b_col + m_prev
        m_t = jnp.maximum(inter, jnp.max(dmat, axis=-1, keepdims=True))
        w_intra = jnp.exp(dmat - m_t)
        w_inter = jnp.exp(inter - m_t)
        qb = q.astype(BF16)
        kb = k.astype(BF16)
        s = lax.dot_general(qb, kb, tdims, preferred_element_type=F32) * w_intra
        c_old = cs_ref[h]
        n_old = ns_ref[h]
        num = (jnp.dot(s.astype(BF16), v, preferred_element_type=F32)
               + w_inter * jnp.dot(qb, c_old.astype(BF16), preferred_element_type=F32))
        qn = jnp.sum(s, axis=-1, keepdims=True) + w_inter * jnp.sum(q * n_old, axis=-1, keepdims=True)
        den = jnp.maximum(jnp.abs(qn), jnp.exp(-m_t))
        hh = num / den
        m_new = m_t[L - 1:L, :]
        b_last = bcum[L - 1:L, M_HEADS + h:M_HEADS + h + 1]
        w_prev = jnp.exp(b_last + m_prev - m_new)
        kw = k * jnp.exp(b_last - b_col + i_col - m_new)
        cs_ref[h] = w_prev * c_old + lax.dot_general(kw.astype(BF16), v, (((0,), (0,)), ((), ())),
                                                     preferred_element_type=F32)
        ns_ref[h] = w_prev * n_old + jnp.sum(kw, axis=0, keepdims=True)
        ms_ref[h] = jnp.broadcast_to(m_new, (1, 128))
        hn = hh * lax.rsqrt(jnp.mean(hh * hh, axis=-1, keepdims=True) + EPS) * gmh_ref[:, h * M_DV:(h + 1) * M_DV]
        hg_ref[:, h * M_DV:(h + 1) * M_DV] = (hn * _sigmoid(o)).astype(BF16)


    @pl.when(c == nc - 1)
    def _():
        cout_ref[...] = cs_ref[...]
        nout_ref[...] = ns_ref[...]
        mout_ref[...] = ms_ref[...]


def _mlstm(za, zi, seq, w_conv, b_conv, bif, g_mh, c0, n0, m0, cv0):
    b = za.shape[0] // seq
    L = min(seq, 256)
    nc = seq // L
    za3 = za.reshape(b, seq, W_A)
    zi3 = zi.reshape(b, seq, W_GATE)
    shared = c0.shape[0] == 1
    bsel = (lambda bi: 0) if shared else (lambda bi: bi)
    kern = functools.partial(_mlstm_kernel, L=L, nc=nc)
    st4 = lambda last2: pl.BlockSpec((None, M_HEADS) + last2, lambda bi, ci: (bsel(bi), 0, 0, 0))
    out4 = lambda last2: pl.BlockSpec((None, M_HEADS) + last2, lambda bi, ci: (bi, 0, 0, 0))
    return pl.pallas_call(
        kern,
        out_shape=(jax.ShapeDtypeStruct((b, seq, M_V), BF16),
                   jax.ShapeDtypeStruct((b, M_HEADS, M_DK, M_DV), F32),
                   jax.ShapeDtypeStruct((b, M_HEADS, 1, M_DK), F32),
                   jax.ShapeDtypeStruct((b, M_HEADS, 1, 128), F32)),
        grid=(b, nc),
        in_specs=[pl.BlockSpec((None, L, W_A), lambda bi, ci: (bi, ci, 0)),
                  pl.BlockSpec((None, L, W_GATE), lambda bi, ci: (bi, ci, 0)),
                  _resident((CONV_W, 2 * M_QK)), _resident((1, 2 * M_QK)), _resident((1, W_GATE)),
                  _resident((1, M_V)),
                  st4((M_DK, M_DV)), st4((1, M_DK)), st4((1, 128)),
                  pl.BlockSpec((None, 8, 2 * M_QK), lambda bi, ci: (bsel(bi), 0, 0))],
        out_specs=(pl.BlockSpec((None, L, M_V), lambda bi, ci: (bi, ci, 0)),
                   out4((M_DK, M_DV)), out4((1, M_DK)), out4((1, 128))),
        scratch_shapes=[pltpu.VMEM((M_HEADS, M_DK, M_DV), F32), pltpu.VMEM((M_HEADS, 1, M_DK), F32),
                        pltpu.VMEM((M_HEADS, 1, 128), F32), pltpu.VMEM((8, 2 * M_QK), F32)],
        compiler_params=_cparams(("parallel", "arbitrary")),
        name="mlstm",
    )(za3, zi3, w_conv, b_conv, bif, g_mh, c0, n0, m0, cv0)


def _attn_kernel(cq_ref, ct_ref, st_ref, wn_ref, wr_ref, wrs_ref, wuk_ref, wuv_ref,
                 plat_ref, pkr_ref, klat_ref, kkr_ref, o_ref,
                 ql_ref, qr_ref, s0_ref, s_ref, mrun_ref, mb_ref, lrun_ref, acc_ref,
                 *, tq, tk, n_prefix, n_kt, causal):
    i = pl.program_id(1)
    r = A_HEADS * tq
    tdims = (((1,), (1,)), ((), ()))
    lane_chunks = tk // 128

    cq = cq_ref[...]
    qn_all = jnp.dot(cq, wn_ref[...], preferred_element_type=F32)
    qr_all = jnp.dot(cq, wr_ref[...], preferred_element_type=F32)
    qrs_all = jnp.dot(cq, wrs_ref[...], preferred_element_type=F32)
    ct = ct_ref[...]
    st = st_ref[...]
    for h in range(A_HEADS):
        qn = qn_all[:, h * NOPE_DIM:(h + 1) * NOPE_DIM].astype(BF16)
        ql = jnp.dot(qn, wuk_ref[h], preferred_element_type=F32) * QUERY_SCALE
        ql_ref[h * tq:(h + 1) * tq, :] = ql.astype(BF16)
        qr = (qr_all[:, h * 128:(h + 1) * 128] * ct + qrs_all[:, h * 128:(h + 1) * 128] * st) * QUERY_SCALE
        qr_ref[h * tq:(h + 1) * tq, :] = qr.astype(BF16)

    def scores(kl, kk):
        return (lax.dot_general(ql_ref[...], kl, tdims, preferred_element_type=F32)
                + lax.dot_general(qr_ref[:, 0:ROPE_DIM], kk, tdims, preferred_element_type=F32))

    pl_b = plat_ref[...].astype(BF16)
    s0 = scores(pl_b, pkr_ref[...].astype(BF16))
    pcol = lax.broadcasted_iota(jnp.int32, s0.shape, 1)
    s0 = jnp.where(pcol < n_prefix, s0, -jnp.inf)
    s0_ref[...] = s0
    mrun_ref[...] = s0

    def score_block(j, width, masked):
        start = pl.multiple_of(j * tk, tk)
        kl = klat_ref[pl.ds(start, width), :].astype(BF16)
        kk = kkr_ref[pl.ds(start, width), :].astype(BF16)
        s = scores(kl, kk)
        if masked:
            qpos = i * tq + (lax.broadcasted_iota(jnp.int32, s.shape, 0) & (tq - 1))
            kpos = start + lax.broadcasted_iota(jnp.int32, s.shape, 1)
            shift = CHUNK.bit_length() - 1
            s = jnp.where(jnp.right_shift(kpos, shift) <= jnp.right_shift(qpos, shift), s, -jnp.inf)
        s_ref[j, :, 0:width] = s
        mr = mrun_ref[...]
        for c in range(width // 128):
            mr = jnp.maximum(mr, s[:, c * 128:(c + 1) * 128])
        mrun_ref[...] = mr

    def score_body(j, carry):
        score_block(j, tk, False)
        return carry

    if causal:
        last = (i * tq) // tk
        groups = (i * tq - last * tk + tq + 127) // 128
        lax.fori_loop(0, last, score_body, 0)
        for v in range(1, lane_chunks + 1):
            @pl.when(groups == v)
            def _():
                score_block(last, v * 128, True)
        n_full = last
    else:
        lax.fori_loop(0, n_kt, score_body, 0)
        n_full = n_kt

    mb = jnp.broadcast_to(jnp.max(mrun_ref[...], axis=-1, keepdims=True), (r, 128))
    mb_ref[...] = mb
    p0 = jnp.exp2(s0_ref[...] - mb)
    lrun_ref[...] = p0
    acc_ref[...] = jnp.dot(p0.astype(BF16), pl_b, preferred_element_type=F32)

    def value_block(j, width):
        start = pl.multiple_of(j * tk, tk)
        kl = klat_ref[pl.ds(start, width), :].astype(BF16)
        mbw = jnp.concatenate([mb_ref[...]] * (width // 128), axis=1)
        p = jnp.exp2(s_ref[j, :, 0:width] - mbw)
        lr = lrun_ref[...]
        for c in range(width // 128):
            lr = lr + p[:, c * 128:(c + 1) * 128]
        lrun_ref[...] = lr
        acc_ref[...] += jnp.dot(p.astype(BF16), kl, preferred_element_type=F32)

    def value_body(j, carry):
        value_block(j, tk)
        return carry

    lax.fori_loop(0, n_full, value_body, 0)
    if causal:
        for v in range(1, lane_chunks + 1):
            @pl.when(groups == v)
            def _():
                value_block(last, v * 128)

    o = acc_ref[...] / jnp.sum(lrun_ref[...], axis=-1, keepdims=True)
    for h in range(A_HEADS):
        oh = o[h * tq:(h + 1) * tq, :].astype(BF16)
        o_ref[:, h * V_HEAD:(h + 1) * V_HEAD] = jnp.dot(oh, wuv_ref[h], preferred_element_type=F32).astype(BF16)


def _attention(cqn, seq, ctab, stab, wn, wr, wrs, wuk, wuv, plat, pkr, n_prefix, klat, kkr, causal):
    b = cqn.shape[0] // seq
    tq = min(seq, 128)
    nq = seq // tq
    tkeys = klat.shape[1]
    tk = 512
    n_kt = tkeys // tk
    r = A_HEADS * tq
    cq3 = cqn.reshape(b, seq, Q_LORA)
    psel = (lambda bi: 0) if plat.shape[0] == 1 else (lambda bi: bi)
    kern = functools.partial(_attn_kernel, tq=tq, tk=tk, n_prefix=n_prefix, n_kt=n_kt, causal=causal)
    tab = pl.BlockSpec((tq, 128), lambda bi, qi: (qi, 0))
    out = pl.pallas_call(
        kern,
        out_shape=jax.ShapeDtypeStruct((b, seq, A_HEADS * V_HEAD), BF16),
        grid=(b, nq),
        in_specs=[pl.BlockSpec((None, tq, Q_LORA), lambda bi, qi: (bi, qi, 0)), tab, tab,
                  _resident((Q_LORA, A_HEADS * NOPE_DIM)), _resident((Q_LORA, A_HEADS * 128)),
                  _resident((Q_LORA, A_HEADS * 128)), _resident((A_HEADS, NOPE_DIM, KV_LORA)),
                  _resident((A_HEADS, KV_LORA, V_HEAD)),
                  pl.BlockSpec((None, PREFIX_PAD, KV_LORA), lambda bi, qi: (psel(bi), 0, 0)),
                  pl.BlockSpec((None, PREFIX_PAD, ROPE_DIM), lambda bi, qi: (psel(bi), 0, 0)),
                  pl.BlockSpec((None, tkeys, KV_LORA), lambda bi, qi: (bi, 0, 0)),
                  pl.BlockSpec((None, tkeys, ROPE_DIM), lambda bi, qi: (bi, 0, 0))],
        out_specs=pl.BlockSpec((None, tq, A_HEADS * V_HEAD), lambda bi, qi: (bi, qi, 0)),
        scratch_shapes=[pltpu.VMEM((r, KV_LORA), BF16), pltpu.VMEM((r, 128), BF16),
                        pltpu.VMEM((r, PREFIX_PAD), F32), pltpu.VMEM((n_kt, r, tk), F32),
                        pltpu.VMEM((r, 128), F32), pltpu.VMEM((r, 128), F32), pltpu.VMEM((r, 128), F32),
                        pltpu.VMEM((r, KV_LORA), F32)],
        compiler_params=_cparams(("parallel", "arbitrary")),
        name="attention",
    )(cq3, ctab, stab, wn, wr, wrs, wuk, wuv, plat, pkr, klat, kkr)
    return out.reshape(b * seq, A_HEADS * V_HEAD)


ROW_TILE = (D_MODEL // 128, 128)


def _store_row_tiles(ref, rows):
    n = rows.shape[0]
    for s in range(ROW_TILE[0]):
        ref[pl.ds(s, n, stride=ROW_TILE[0]), :] = rows[:, s * 128:(s + 1) * 128]


def _load_row_tiles(ref, n):
    return [ref[pl.ds(s, n, stride=ROW_TILE[0]), :] for s in range(ROW_TILE[0])]


def _merge_kernel(hg_ref, ov_ref, gab_ref, x_ref, wa_ref, wb_ref, wo_ref, gf_ref, wrh_ref, wrl_ref, br_ref,
                  x1_ref, x2_ref, ti_ref, tg_ref, tr_ref, cnt_ref, carry_ref):
    i = pl.program_id(0)

    @pl.when(i == 0)
    def _():
        carry_ref[...] = jnp.zeros_like(carry_ref)

    tm = x_ref.shape[0]
    ya = jnp.dot(hg_ref[...], wa_ref[...], preferred_element_type=F32)
    yb = jnp.dot(ov_ref[...], wb_ref[...], preferred_element_type=F32)
    mixed = (_sigmoid(gab_ref[:, 0:D_MODEL].astype(F32)) * ya
             + _sigmoid(gab_ref[:, D_MODEL:2 * D_MODEL].astype(F32)) * yb)
    x1 = x_ref[...] + jnp.dot(mixed.astype(BF16), wo_ref[...], preferred_element_type=F32)
    x1_ref[...] = x1
    x2 = x1 * lax.rsqrt(jnp.mean(x1 * x1, axis=-1, keepdims=True) + EPS) * gf_ref[...]
    _store_row_tiles(x2_ref, x2)

    lane = lax.broadcasted_iota(jnp.int32, (tm, 128), 1)
    lane_f = lane.astype(F32)
    x2_hi = x2.astype(BF16)
    x2_lo = (x2 - x2_hi.astype(F32)).astype(BF16)
    logits = (jnp.dot(x2_hi, wrh_ref[...], preferred_element_type=F32)
              + jnp.dot(x2_lo, wrh_ref[...], preferred_element_type=F32)
              + jnp.dot(x2_hi, wrl_ref[...], preferred_element_type=F32) + br_ref[...])
    cur = jnp.where(lane < N_EXPERTS, logits, -jnp.inf)
    vals, idxs = [], []
    for _ in range(TOP_K):
        mx = jnp.max(cur, axis=-1, keepdims=True)
        idx = jnp.min(jnp.where(cur == mx, lane_f, 128.0), axis=-1, keepdims=True)
        vals.append(mx)
        idxs.append(idx)
        cur = jnp.where(lane_f == idx, -jnp.inf, cur)
    es = [jnp.exp(v - vals[0]) for v in vals]
    tot = es[0] + es[1] + es[2] + es[3]

    onehots = [(lane_f == idx) for idx in idxs]
    cnt = jnp.zeros((tm, 128), F32)
    for oh in onehots:
        cnt = cnt + jnp.where(oh, 1.0, 0.0)
    rr = lax.broadcasted_iota(jnp.int32, (tm, tm), 0)
    cc = lax.broadcasted_iota(jnp.int32, (tm, tm), 1)
    before = jnp.dot((rr > cc).astype(BF16), cnt.astype(BF16), preferred_element_type=F32) + carry_ref[0:1, :]
    ti = jnp.zeros((tm, 128), F32)
    tg = jnp.zeros((tm, 128), F32)
    tr = jnp.zeros((tm, 128), F32)
    for k in range(TOP_K):
        rank = jnp.sum(jnp.where(onehots[k], before, 0.0), axis=-1, keepdims=True)
        ti = jnp.where(lane == k, idxs[k], ti)
        tg = jnp.where(lane == k, es[k] / tot, tg)
        tr = jnp.where(lane == k, rank, tr)
    ti_ref[...] = ti.astype(jnp.int32)
    tg_ref[...] = tg
    tr_ref[...] = tr.astype(jnp.int32)
    new_carry = carry_ref[0:1, :] + jnp.sum(cnt, axis=0, keepdims=True)
    carry_ref[...] = jnp.broadcast_to(new_carry, carry_ref.shape)
    cnt_ref[...] = jnp.broadcast_to(new_carry, cnt_ref.shape).astype(jnp.int32)


def _merge_route(hg, ov, gab, x2d, wa, wb, wo, g_ffn, w_router_hi, w_router_lo, b_router):
    m = x2d.shape[0]
    tm = min(512, m)
    row = lambda w: pl.BlockSpec((tm, w), lambda i: (i, 0))
    sq = _resident((D_MODEL, D_MODEL))
    return pl.pallas_call(
        _merge_kernel,
        out_shape=(jax.ShapeDtypeStruct((m, D_MODEL), F32), jax.ShapeDtypeStruct((m * ROW_TILE[0], 128), F32),
                   jax.ShapeDtypeStruct((m, 128), jnp.int32), jax.ShapeDtypeStruct((m, 128), F32),
                   jax.ShapeDtypeStruct((m, 128), jnp.int32), jax.ShapeDtypeStruct((8, 128), jnp.int32)),
        grid=(m // tm,),
        in_specs=[row(M_V), row(A_HEADS * V_HEAD), row(W_G), row(D_MODEL), sq, sq, sq,
                  _resident((1, D_MODEL)), _resident((D_MODEL, 128)), _resident((D_MODEL, 128)),
                  _resident((1, 128))],
        out_specs=(row(D_MODEL), pl.BlockSpec((tm * ROW_TILE[0], 128), lambda i: (i, 0)), row(128), row(128),
                   row(128), pl.BlockSpec((8, 128), lambda i: (0, 0))),
        scratch_shapes=[pltpu.VMEM((8, 128), F32)],
        compiler_params=_cparams(("arbitrary",)),
        name="merge_route",
    )(hg, ov, gab, x2d, wa, wb, wo, g_ffn, w_router_hi, w_router_lo, b_router)


def _dispatch_kernel(zt_ref, dest_ref, x_ref, *refs):
    xs_ref, zbuf_ref, sem, zsem = refs[-4:]
    i = pl.program_id(0)
    tr = x_ref.shape[0]
    tm = zbuf_ref.shape[0]

    @pl.when((i == 0) & (zt_ref[0] > 0))
    def _():
        zbuf_ref[...] = jnp.zeros_like(zbuf_ref)

        def start_zero(j, carry):
            start = pl.multiple_of(zt_ref[1 + j], tm)
            pltpu.make_async_copy(zbuf_ref, xs_ref.at[pl.ds(start, tm)], zsem).start()
            return carry

        def wait_zero(j, carry):
            pltpu.make_async_copy(zbuf_ref, xs_ref.at[pl.ds(0, tm)], zsem).wait()
            return carry

        lax.fori_loop(0, zt_ref[0], start_zero, 0)
        lax.fori_loop(0, zt_ref[0], wait_zero, 0)

    def issue(r, carry):
        for k in range(TOP_K):
            pltpu.make_async_copy(x_ref.at[r], xs_ref.at[dest_ref[0, 0, r * TOP_K + k]],
                                  sem).start(priority=k % 2)
        return carry

    lax.fori_loop(0, tr, issue, 0, unroll=8)
    for _ in range(TOP_K):
        pltpu.make_async_copy(x_ref, xs_ref.at[pl.ds(0, tr)], sem).wait()


def _dispatch(x2t, dest, zero_tiles, n_rows, xs_prev=None):
    m = x2t.shape[0] // ROW_TILE[0]
    tr = min(256, m)
    nt = m // tr
    dest3 = dest.reshape(nt, 1, tr * TOP_K)
    in_specs = [pl.BlockSpec((1, 1, tr * TOP_K), lambda i, zt: (i, 0, 0), memory_space=pltpu.SMEM),
                pl.BlockSpec((tr,) + ROW_TILE, lambda i, zt: (i, 0, 0))]
    args = [zero_tiles, dest3, x2t.reshape((m,) + ROW_TILE)]
    aliases = {}
    if xs_prev is not None:
        in_specs.append(pl.BlockSpec(memory_space=pl.ANY))
        args.append(xs_prev)
        aliases = {3: 0}
    grid_spec = pltpu.PrefetchScalarGridSpec(
        num_scalar_prefetch=1,
        grid=(nt,),
        in_specs=in_specs,
        out_specs=pl.BlockSpec(memory_space=pl.ANY),
        scratch_shapes=[pltpu.VMEM((MOE_TILE,) + ROW_TILE, F32), pltpu.SemaphoreType.DMA(()),
                        pltpu.SemaphoreType.DMA(())],
    )
    return pl.pallas_call(
        _dispatch_kernel,
        out_shape=jax.ShapeDtypeStruct((n_rows,) + ROW_TILE, F32),
        grid_spec=grid_spec,
        input_output_aliases=aliases,
        compiler_params=_cparams(("arbitrary",)),
        name="moe_dispatch",
    )(*args)


def _expert_kernel(te_ref, nu_ref, xs_ref, wgu_ref, bgu_ref, wdn_ref, bdn_ref, ys_ref, wgu_b, wdn_b):
    i = pl.program_id(0)
    prev = te_ref[jnp.maximum(i - 1, 0)]

    @pl.when(i < nu_ref[0])
    def _():
        @pl.when((i == 0) | (te_ref[i] != prev))
        def _():
            rc = 128
            for c in range(D_MODEL // rc):
                wgu_b[c * rc:(c + 1) * rc, :] = wgu_ref[c * rc:(c + 1) * rc, :].astype(BF16)
                wdn_b[c * rc:(c + 1) * rc, :] = wdn_ref[c * rc:(c + 1) * rc, :].astype(BF16)

        tm = xs_ref.shape[0] // ROW_TILE[0]
        xb = jnp.concatenate([c.astype(BF16) for c in _load_row_tiles(xs_ref, tm)], axis=1)
        fc = 512
        y = jnp.zeros((tm, D_MODEL), F32) + bdn_ref[...]
        for f0 in range(0, D_FF, fc):
            def proj(c0):
                return jnp.dot(xb, wgu_b[:, c0:c0 + fc], preferred_element_type=F32) + bgu_ref[:, c0:c0 + fc]
            gate = jnp.minimum(proj(f0), SWIGLU_LIMIT)
            up = jnp.clip(proj(D_FF + f0), -SWIGLU_LIMIT, SWIGLU_LIMIT)
            hid = (up + 1.0) * gate * _sigmoid(SWIGLU_ALPHA * gate)
            y = y + jnp.dot(hid.astype(BF16), wdn_b[f0:f0 + fc, :], preferred_element_type=F32)
        _store_row_tiles(ys_ref, y)

    @pl.when(i >= nu_ref[0])
    def _():
        ys_ref[...] = jnp.zeros_like(ys_ref)


def _experts(xs, tile_expert, n_used, w_gu, b_gu, w_dn, b_dn):
    n_rows, w = xs.shape
    tm = MOE_TILE * ROW_TILE[0]
    nt = n_rows // tm
    tile = lambda i, te, nu: (jnp.minimum(i, jnp.maximum(nu[0] - 1, 0)), 0)
    grid_spec = pltpu.PrefetchScalarGridSpec(
        num_scalar_prefetch=2,
        grid=(nt,),
        in_specs=[pl.BlockSpec((tm, w), tile),
                  pl.BlockSpec((None, D_MODEL, 2 * D_FF), lambda i, te, nu: (te[i], 0, 0)),
                  pl.BlockSpec((None, 1, 2 * D_FF), lambda i, te, nu: (te[i], 0, 0)),
                  pl.BlockSpec((None, D_FF, D_MODEL), lambda i, te, nu: (te[i], 0, 0)),
                  pl.BlockSpec((None, 1, D_MODEL), lambda i, te, nu: (te[i], 0, 0))],
        out_specs=pl.BlockSpec((tm, w), lambda i, te, nu: (i, 0)),
        scratch_shapes=[pltpu.VMEM((D_MODEL, 2 * D_FF), BF16), pltpu.VMEM((D_FF, D_MODEL), BF16)],
    )
    return pl.pallas_call(
        _expert_kernel,
        out_shape=jax.ShapeDtypeStruct((n_rows, w), F32),
        grid_spec=grid_spec,
        compiler_params=_cparams(("arbitrary",)),
        name="moe_experts",
    )(tile_expert, n_used, xs, w_gu, b_gu, w_dn, b_dn)


def _combine_kernel(dcur_ref, dnext_ref, x1_ref, tg_ref, gfin_ref, ys_ref, y_ref, buf_ref, sem_ref, *, nt):
    i = pl.program_id(0)
    tc = x1_ref.shape[0]

    def issue(dref, slot):
        def body(r, carry):
            for k in range(TOP_K):
                src0 = pl.multiple_of(dref[0, 0, r * TOP_K + k] * ROW_TILE[0], ROW_TILE[0])
                dst0 = pl.multiple_of(r * ROW_TILE[0], ROW_TILE[0])
                pltpu.make_async_copy(ys_ref.at[pl.ds(src0, ROW_TILE[0]), :],
                                      buf_ref.at[slot, k, pl.ds(dst0, ROW_TILE[0]), :],
                                      sem_ref.at[slot]).start(priority=k % 2)
            return carry
        lax.fori_loop(0, tc, body, 0, unroll=8)

    slot = i % 2

    @pl.when(i == 0)
    def _():
        issue(dcur_ref, 0)

    @pl.when(i + 1 < nt)
    def _():
        issue(dnext_ref, 1 - slot)

    for k in range(TOP_K):
        pltpu.make_async_copy(ys_ref.at[pl.ds(0, tc * ROW_TILE[0]), :], buf_ref.at[slot, k],
                              sem_ref.at[slot]).wait()

    gates = [tg_ref[:, k:k + 1] for k in range(TOP_K)]
    chunks = []
    ssq = jnp.zeros((tc, 1), F32)
    for s in range(ROW_TILE[0]):
        xc = x1_ref[:, s * 128:(s + 1) * 128]
        for k in range(TOP_K):
            xc = xc + gates[k] * buf_ref[slot, k, pl.ds(s, tc, stride=ROW_TILE[0]), :]
        chunks.append(xc)
        ssq = ssq + jnp.sum(xc * xc, axis=-1, keepdims=True)
    inv = lax.rsqrt(ssq / D_MODEL + EPS)
    for s in range(ROW_TILE[0]):
        y_ref[:, s * 128:(s + 1) * 128] = chunks[s] * inv * gfin_ref[:, s * 128:(s + 1) * 128]


def _combine(dest, x1, tg, g_final, ys):
    m = x1.shape[0]
    tc = min(256, m)
    nt = m // tc
    dest3 = dest.reshape(nt, 1, tc * TOP_K)
    kern = functools.partial(_combine_kernel, nt=nt)
    smem = lambda f: pl.BlockSpec((1, 1, tc * TOP_K), f, memory_space=pltpu.SMEM)
    return pl.pallas_call(
        kern,
        out_shape=jax.ShapeDtypeStruct((m, D_MODEL), F32),
        grid=(nt,),
        in_specs=[smem(lambda i: (i, 0, 0)), smem(lambda i: (jnp.minimum(i + 1, nt - 1), 0, 0)),
                  pl.BlockSpec((tc, D_MODEL), lambda i: (i, 0)), pl.BlockSpec((tc, 128), lambda i: (i, 0)),
                  _resident((1, D_MODEL)), pl.BlockSpec(memory_space=pl.ANY)],
        out_specs=pl.BlockSpec((tc, D_MODEL), lambda i: (i, 0)),
        scratch_shapes=[pltpu.VMEM((2, TOP_K, tc * ROW_TILE[0], 128), F32), pltpu.SemaphoreType.DMA((2,))],
        compiler_params=_cparams(("arbitrary",)),
        name="moe_combine",
    )(dest3, dest3, x1, tg, g_final, ys)


def _rope_tables(pos):
    half = ROPE_DIM // 2
    freqs = ROPE_THETA ** (-jnp.arange(half, dtype=F32) / half)
    ang = pos.astype(F32)[:, None] * freqs[None, :]
    cos, sin = jnp.cos(ang), jnp.sin(ang)
    return jnp.concatenate([cos, cos], axis=-1), jnp.concatenate([-sin, sin], axis=-1)


def _tile_rows(t, rows):
    return t if t.shape[0] >= rows else jnp.tile(t, (rows // t.shape[0], 1))


def _pad_lanes(t, width):
    return jnp.pad(t, ((0, 0), (0, width - t.shape[1])))


def _moe(streams, g_final, w_gu, b_gu, w_dn, b_dn):
    tm = MOE_TILE
    a = sum(s[0].shape[0] for s in streams) * TOP_K
    n_tiles = -(-a // tm) + N_EXPERTS
    counts = [s[5][0, :N_EXPERTS] for s in streams]
    total = sum(counts)
    pcounts = (total + tm - 1) // tm * tm
    pend = jnp.cumsum(pcounts)
    n_used = (pend[-1] // tm).astype(jnp.int32)
    tiles = jnp.minimum(jnp.arange(n_tiles, dtype=jnp.int32), n_used - 1) * tm
    tile_expert = jnp.minimum(jnp.sum(pend[None, :] <= tiles[:, None], axis=-1), N_EXPERTS - 1).astype(jnp.int32)
    eids = jnp.arange(N_EXPERTS, dtype=jnp.int32)
    base = pend - pcounts
    rest_tiles = (a - streams[0][0].shape[0] * TOP_K) // tm
    first = (base + counts[0]) // tm
    cand = first[:, None] + jnp.arange(rest_tiles + 2, dtype=jnp.int32)[None, :]
    cand = jnp.where((cand < (pend // tm)[:, None]) & (pcounts[:, None] > 0), cand, -1)
    tail = n_used + jnp.arange(N_EXPERTS, dtype=jnp.int32)
    cand = jnp.concatenate([cand.reshape(-1), jnp.where(tail < n_tiles, tail, -1)])
    n_zero = jnp.sum(cand >= 0).astype(jnp.int32)
    zlist = (-jnp.sort(-cand))[:rest_tiles + 3 * N_EXPERTS] * tm
    zero_tiles = jnp.concatenate([n_zero[None], zlist]).astype(jnp.int32)
    dests, xs = [], None
    for s, (_, x2t, ti, _, tr, _) in enumerate(streams):
        dest = tr[:, :TOP_K] + jnp.sum(jnp.where(ti[:, :TOP_K, None] == eids, base, 0), axis=-1)
        dests.append(dest.astype(jnp.int32))
        zt = zero_tiles if s == 0 else jnp.zeros_like(zero_tiles)
        xs = _dispatch(x2t, dests[-1], zt, n_tiles * tm, xs)
        base = base + counts[s]
    ys = _experts(xs.reshape(n_tiles * tm * ROW_TILE[0], 128), tile_expert, n_used.reshape(1),
                  w_gu, b_gu, w_dn, b_dn)
    return [_combine(dest, s[0], s[3], g_final, ys) for dest, s in zip(dests, streams)]


def kernel(x_prompt, x_sample, cache_kv_latent, cache_k_rope, state_mlstm_C, state_mlstm_n, state_mlstm_m, state_conv, meta_tokens, g_mix_norm, w_in, b_if, w_conv, b_conv, g_mh_norm, w_proj_a, g_q_norm, g_kv_norm, w_uq, w_uk, w_uv, w_proj_b, w_out, g_ffn_norm, w_router, b_router, w_gate_up, b_gate_up, w_down, b_down, g_final_norm):
    bsz, seq = x_prompt.shape[0], x_prompt.shape[1]
    dbs, dseq = x_sample.shape[0], x_sample.shape[1]
    past = cache_kv_latent.shape[2]
    assert w_in.shape[0] == 1, "single-layer trunk"
    l = 0

    wi = w_in[l]
    o_gate = 2 * M_QK + 2 * M_V
    o_cq = o_gate + 2 * M_HEADS
    o_ckv = o_cq + Q_LORA
    o_kr = o_ckv + KV_LORA
    o_g = o_kr + ROPE_DIM
    swap = np.concatenate([np.arange(ROPE_DIM // 2, ROPE_DIM), np.arange(ROPE_DIM // 2)])
    w_kr = wi[:, o_kr:o_g]
    w_cat = jnp.concatenate([wi[:, :o_gate], _pad_lanes(wi[:, o_gate:o_cq], W_GATE), wi[:, o_cq:o_kr], w_kr,
                             w_kr[:, swap], wi[:, o_g:]], axis=1).astype(BF16)
    bif = _pad_lanes(b_if[l][None, :], W_GATE)
    uq = w_uq[l].reshape(Q_LORA, A_HEADS, NOPE_DIM + ROPE_DIM)
    wn = uq[:, :, :NOPE_DIM].reshape(Q_LORA, A_HEADS * NOPE_DIM).astype(BF16)
    uq_r = uq[:, :, NOPE_DIM:]
    pad_r = lambda t: jnp.pad(t, ((0, 0), (0, 0), (0, 128 - ROPE_DIM))).reshape(Q_LORA, A_HEADS * 128).astype(BF16)
    wr, wrs = pad_r(uq_r), pad_r(uq_r[:, :, swap])
    wuk = jnp.transpose(w_uk[l], (1, 2, 0)).astype(BF16)
    wuv = jnp.transpose(w_uv[l], (1, 0, 2)).astype(BF16)
    wa, wb, wo = w_proj_a[l].astype(BF16), w_proj_b[l].astype(BF16), w_out[l].astype(BF16)
    w_rt = _pad_lanes(w_router[l], 128)
    w_rt_hi = w_rt.astype(BF16)
    w_rt_lo = (w_rt - w_rt_hi.astype(F32)).astype(BF16)
    b_rt = _pad_lanes(b_router[l][None, :], 128)
    g_mix, g_q, g_kv = g_mix_norm[l][None, :], g_q_norm[l][None, :], g_kv_norm[l][None, :]
    g_mh, g_ffn, g_fin = g_mh_norm[l][None, :], g_ffn_norm[l][None, :], g_final_norm[None, :]
    wcv, bcv = w_conv[l], b_conv[l][None, :]
    w_gu, b_gu = w_gate_up[l], b_gate_up[l][:, None, :]
    w_dn, b_dn = w_down[l], b_down[l][:, None, :]

    ct_m, st_m = _rope_tables(jnp.arange(N_META))
    ct_p, st_p = _rope_tables(N_META + jnp.arange(seq))
    ct_s, st_s = _rope_tables(N_META + past + jnp.arange(dseq))

    def stream_tables(ct, st, m_rows):
        rows = max(ct.shape[0], min(512, m_rows))
        return _tile_rows(ct, rows), _tile_rows(st, rows)

    def prefix(lat, kr):
        pad = ((0, 0), (0, PREFIX_PAD - lat.shape[1]), (0, 0))
        return jnp.pad(lat, pad), jnp.pad(kr, pad)

    za_m, zi_m, _, lat_m, kr_m, _, tail_m = _inproj(meta_tokens, N_META, g_mix, w_cat, g_q, g_kv, ct_m, st_m)
    npad = CHUNK - N_META
    za_mp = jnp.concatenate([jnp.zeros((npad, W_A), BF16), za_m], axis=0)
    lane = jnp.arange(W_GATE)
    neutral = jnp.where(lane < M_HEADS, -1e30, jnp.where(lane < 2 * M_HEADS, 1e30, 0.0)).astype(F32)
    zi_mp = jnp.concatenate([jnp.broadcast_to(neutral, (npad, W_GATE)), zi_m], axis=0)
    zeros_state = (jnp.zeros((1, M_HEADS, M_DK, M_DV), F32), jnp.zeros((1, M_HEADS, 1, M_DK), F32),
                   jnp.zeros((1, M_HEADS, 1, 128), F32), jnp.zeros((1, 8, 2 * M_QK), F32))
    _, c_m, n_m, m_m = _mlstm(za_mp, zi_mp, CHUNK, wcv, bcv, bif, g_mh, *zeros_state)

    def route(x2d, hg, ov, gab):
        return _merge_route(hg.reshape(x2d.shape[0], M_V), ov, gab, x2d, wa, wb, wo, g_ffn, w_rt_hi, w_rt_lo, b_rt)

    xp = x_prompt.reshape(bsz * seq, D_MODEL)
    ctp, stp = stream_tables(ct_p, st_p, bsz * seq)
    za, zi, cqn, lat, kr, gab, tail = _inproj(xp, seq, g_mix, w_cat, g_q, g_kv, ctp, stp)
    hg, c_p, n_p, m_p = _mlstm(za, zi, seq, wcv, bcv, bif, g_mh, c_m, n_m, m_m, tail_m)
    plat, pkr = prefix(lat_m[None], kr_m[None])
    ctq, stq = _pad_lanes(ct_p, 128), _pad_lanes(st_p, 128)
    ov = _attention(cqn, seq, ctq, stq, wn, wr, wrs, wuk, wuv, plat, pkr, N_META,
                    lat.reshape(bsz, seq, KV_LORA), kr.reshape(bsz, seq, ROPE_DIM), True)
    routed_p = route(xp, hg, ov, gab)

    xs2 = x_sample.reshape(dbs * dseq, D_MODEL)
    cts, sts = stream_tables(ct_s, st_s, dbs * dseq)
    za, zi, cqn, lat_s, kr_s, gab, tail_s = _inproj(xs2, dseq, g_mix, w_cat, g_q, g_kv, cts, sts)
    cv0 = jnp.pad(state_conv[l], ((0, 0), (8 - (CONV_W - 1), 0), (0, 0)))
    m0 = jnp.broadcast_to(state_mlstm_m[l][:, :, None, None], (dbs, M_HEADS, 1, 128))
    hg, c_s, n_s, m_s = _mlstm(za, zi, dseq, wcv, bcv, bif, g_mh, state_mlstm_C[l],
                               state_mlstm_n[l][:, :, None, :], m0, cv0)
    plat, pkr = prefix(lat_s.reshape(dbs, dseq, KV_LORA), kr_s.reshape(dbs, dseq, ROPE_DIM))
    ov = _attention(cqn, dseq, _pad_lanes(ct_s, 128), _pad_lanes(st_s, 128), wn, wr, wrs, wuk, wuv, plat, pkr,
                    dseq, cache_kv_latent[l], cache_k_rope[l], False)
    routed_s = route(xs2, hg, ov, gab)

    y_p, y_s = _moe([routed_p, routed_s], g_fin, w_gu, b_gu, w_dn, b_dn)
    y_prompt = y_p.reshape(bsz, seq, D_MODEL)
    y_sample = y_s.reshape(dbs, dseq, D_MODEL)

    def with_meta(meta_rows, frames, width):
        return jnp.concatenate([jnp.broadcast_to(meta_rows[None], (bsz, N_META, width)),
                                frames.reshape(bsz, seq, width)], axis=1)[None]

    p_lat = with_meta(lat_m, lat, KV_LORA)
    p_kr = with_meta(kr_m, kr, ROPE_DIM)
    tail3 = lambda t: t[:, 8 - (CONV_W - 1):, :][None]
    return (y_prompt, y_sample, p_lat, p_kr, c_p[None], n_p[:, :, 0, :][None], m_p[:, :, 0, 0][None], tail3(tail),
            lat_s.reshape(dbs, dseq, KV_LORA)[None], kr_s.reshape(dbs, dseq, ROPE_DIM)[None], c_s[None],
            n_s[:, :, 0, :][None], m_s[:, :, 0, 0][None], tail3(tail_s))
```

```python
import functools

import jax
import jax.numpy as jnp
import numpy as np
from jax import lax
from jax.experimental import pallas as pl
from jax.experimental.pallas import tpu as pltpu

F32 = jnp.float32
BF16 = jnp.bfloat16
HIGHEST = lax.Precision.HIGHEST

D_MODEL = 1024
N_META = 16
CHUNK = 64
M_HEADS = 4
M_DK = 256
M_DV = 256
M_QK = M_HEADS * M_DK
M_V = M_HEADS * M_DV
CONV_W = 4
A_HEADS = 8
NOPE_DIM = 128
ROPE_DIM = 64
V_HEAD = 128
Q_LORA = 384
KV_LORA = 256
ROPE_THETA = 10000.0
ATTN_SCALE = (NOPE_DIM + ROPE_DIM) ** -0.5
QUERY_SCALE = ATTN_SCALE * float(np.log2(np.e))
N_EXPERTS = 32
TOP_K = 4
D_FF = 1024
SWIGLU_LIMIT = 7.0
SWIGLU_ALPHA = 1.702
EPS = 1e-6

COL_A = 0
W_A = 2 * M_QK + 2 * M_V
COL_GATE = COL_A + W_A
W_GATE = 128
COL_B = COL_GATE + W_GATE
W_B = Q_LORA + KV_LORA + 2 * ROPE_DIM
COL_G = COL_B + W_B
W_G = 2 * D_MODEL
W_IN_COLS = COL_G + W_G

V7X_VMEM_BYTES = 64 * 2**20
VMEM_LIMIT = V7X_VMEM_BYTES - 8 * 2**20

PREFIX_PAD = 128
MOE_TILE = 512


def _sigmoid(x):
    return 1.0 / (1.0 + jnp.exp(-x))


def _cparams(sem):
    return pltpu.CompilerParams(dimension_semantics=sem, vmem_limit_bytes=VMEM_LIMIT)


def _resident(shape):
    nd = len(shape)
    return pl.BlockSpec(shape, lambda *_: (0,) * nd, pipeline_mode=pl.Buffered(1))


def _inproj_kernel(x_ref, g_ref, w_ref, gq_ref, gkv_ref, ct_ref, st_ref,
                   za_ref, zi_ref, cq_ref, lat_ref, kr_ref, gab_ref, tail_ref, xn_ref, *, seq, spt):
    x = x_ref[...]
    xn = x * lax.rsqrt(jnp.mean(x * x, axis=-1, keepdims=True) + EPS) * g_ref[...]
    xn_ref[...] = xn.astype(BF16)
    xb = xn_ref[...]
    rows = x.shape[0]
    for c in range(W_A // 1024):
        acc = jnp.dot(xb, w_ref[:, COL_A + c * 1024:COL_A + (c + 1) * 1024], preferred_element_type=F32)
        za_ref[:, c * 1024:(c + 1) * 1024] = acc.astype(BF16)
        if c * 1024 < 2 * M_QK:
            if spt == 1:
                tail_ref[0, :, c * 1024:(c + 1) * 1024] = acc[rows - 8:rows, :]
            else:
                for s in range(spt):
                    tail_ref[s, :, c * 1024:(c + 1) * 1024] = acc[(s + 1) * seq - 8:(s + 1) * seq, :]
    zi_ref[...] = jnp.dot(xb, w_ref[:, COL_GATE:COL_GATE + W_GATE], preferred_element_type=F32)
    zb = jnp.dot(xb, w_ref[:, COL_B:COL_B + W_B], preferred_element_type=F32)
    cq = zb[:, 0:Q_LORA]
    cq_ref[...] = (cq * lax.rsqrt(jnp.mean(cq * cq, axis=-1, keepdims=True) + EPS) * gq_ref[...]).astype(BF16)
    ckv = zb[:, Q_LORA:Q_LORA + KV_LORA]
    lat_ref[...] = ckv * lax.rsqrt(jnp.mean(ckv * ckv, axis=-1, keepdims=True) + EPS) * gkv_ref[...]
    k_r = zb[:, Q_LORA + KV_LORA:Q_LORA + KV_LORA + ROPE_DIM]
    k_rs = zb[:, Q_LORA + KV_LORA + ROPE_DIM:W_B]
    kr_ref[...] = k_r * ct_ref[...] + k_rs * st_ref[...]
    for c in range(W_G // 1024):
        acc = jnp.dot(xb, w_ref[:, COL_G + c * 1024:COL_G + (c + 1) * 1024], preferred_element_type=F32)
        gab_ref[:, c * 1024:(c + 1) * 1024] = acc.astype(BF16)


def _inproj(x2d, seq, g_mix, w_cat, g_q, g_kv, ctab, stab):
    m = x2d.shape[0]
    nseq = m // seq
    tm = min(512, m)
    spt = max(1, tm // seq)
    tps = max(1, seq // tm)
    nt = m // tm
    kern = functools.partial(_inproj_kernel, seq=seq, spt=spt)
    row = lambda w: pl.BlockSpec((tm, w), lambda i: (i, 0))
    tab = pl.BlockSpec((tm, ROPE_DIM), lambda i: (i % tps, 0))
    if spt == 1:
        tail_spec = pl.BlockSpec((1, 8, 2 * M_QK), lambda i: (i // tps, 0, 0))
    else:
        tail_spec = pl.BlockSpec((spt, 8, 2 * M_QK), lambda i: (i, 0, 0))
    return pl.pallas_call(
        kern,
        out_shape=(jax.ShapeDtypeStruct((m, W_A), BF16), jax.ShapeDtypeStruct((m, W_GATE), F32),
                   jax.ShapeDtypeStruct((m, Q_LORA), BF16), jax.ShapeDtypeStruct((m, KV_LORA), F32),
                   jax.ShapeDtypeStruct((m, ROPE_DIM), F32), jax.ShapeDtypeStruct((m, W_G), BF16),
                   jax.ShapeDtypeStruct((nseq, 8, 2 * M_QK), F32)),
        grid=(nt,),
        in_specs=[row(D_MODEL), _resident((1, D_MODEL)), _resident((D_MODEL, W_IN_COLS)),
                  _resident((1, Q_LORA)), _resident((1, KV_LORA)), tab, tab],
        out_specs=(row(W_A), row(W_GATE), row(Q_LORA), row(KV_LORA), row(ROPE_DIM), row(W_G), tail_spec),
        scratch_shapes=[pltpu.VMEM((tm, D_MODEL), BF16)],
        compiler_params=_cparams(("arbitrary",)),
        name="inproj",
    )(x2d, g_mix, w_cat, g_q, g_kv, ctab, stab)


def _mlstm_kernel(za_ref, zi_ref, wc_ref, bc_ref, bif_ref, gmh_ref, c0_ref, n0_ref, m0_ref, cv0_ref,
                  hg_ref, cout_ref, nout_ref, mout_ref, cs_ref, ns_ref, ms_ref, cbuf_ref, *, L, nc):
    c = pl.program_id(1)

    @pl.when(c == 0)
    def _():
        cs_ref[...] = c0_ref[...]
        ns_ref[...] = n0_ref[...]
        ms_ref[...] = m0_ref[...]
        cbuf_ref[...] = cv0_ref[...]

    g = zi_ref[...] + bif_ref[...]
    lf = jnp.minimum(g, 0.0) - jnp.log(1.0 + jnp.exp(-jnp.abs(g)))
    row = lax.broadcasted_iota(jnp.int32, (L, L), 0)
    col = lax.broadcasted_iota(jnp.int32, (L, L), 1)
    causal = row >= col
    tdims = (((1,), (1,)), ((), ()))

    def dot01(a01, x, dims):
        a = a01.astype(BF16)
        x_hi = x.astype(BF16)
        x_lo = (x - x_hi.astype(F32)).astype(BF16)
        return (lax.dot_general(a, x_hi, dims, preferred_element_type=F32)
                + lax.dot_general(a, x_lo, dims, preferred_element_type=F32))

    bcum = dot01(causal, lf, (((1,), (0,)), ((), ())))
    sub8 = lax.broadcasted_iota(jnp.int32, (8, 128), 0)
    lane8 = lax.broadcasted_iota(jnp.int32, (8, 128), 1)
    i_rows = dot01(lane8 == sub8, g, tdims)
    b_rows = dot01(lane8 == sub8 + M_HEADS, bcum, tdims)

    delays = [(row - col == d).astype(BF16) for d in range(1, CONV_W)]
    row8 = lax.broadcasted_iota(jnp.int32, (8, M_DK), 0)

    def conv_silu(col0):
        cols = slice(col0, col0 + M_DK)
        ub = za_ref[:, cols]
        u = ub.astype(F32)
        prev = cbuf_ref[:, cols]
        acc = bc_ref[:, cols] + u * wc_ref[CONV_W - 1:CONV_W, cols]
        for d in range(1, CONV_W):
            sh = jnp.dot(delays[d - 1], ub, preferred_element_type=F32)
            head = sh[0:8, :] + jnp.where(row8 < d, pltpu.roll(prev, d, axis=0), 0.0)
            sh = jnp.concatenate([head, sh[8:, :]], axis=0)
            acc = acc + sh * wc_ref[CONV_W - 1 - d:CONV_W - d, cols]
        cbuf_ref[:, cols] = u[L - 8:L, :]
        return acc * _sigmoid(acc)

    for h in range(M_HEADS):
        q = conv_silu(h * M_DK)
        k = conv_silu(M_QK + h * M_DK) * (M_DK ** -0.5)
        v = za_ref[:, 2 * M_QK + h * M_DV:2 * M_QK + (h + 1) * M_DV]
        o = za_ref[:, 2 * M_QK + M_V + h * M_DV:2 * M_QK + M_V + (h + 1) * M_DV].astype(F32)
        i_row = i_rows[h:h + 1, :]
        b_row = b_rows[h:h + 1, :]
        i_col = g[:, h:h + 1]
        b_col = bcum[:, M_HEADS + h:M_HEADS + h + 1]
        m_prev = ms_ref[h][:, 0:1]
        dmat = jnp.where(causal, b_col - b_row + i_row, -jnp.inf)
        inter = b_col + m_prev
        m_t = jnp.maximum(inter, jnp.max(dmat, axis=-1, keepdims=True))
        w_intra = jnp.exp(dmat - m_t)
        w_inter = jnp.exp(inter - m_t)
        qb = q.astype(BF16)
        kb = k.astype(BF16)
        s = lax.dot_general(qb, kb, tdims, preferred_element_type=F32) * w_intra
        c_old = cs_ref[h]
        n_old = ns_ref[h]
        num = (jnp.dot(s.astype(BF16), v, preferred_element_type=F32)
               + w_inter * jnp.dot(qb, c_old.astype(BF16), preferred_element_type=F32))
        qn = jnp.sum(s, axis=-1, keepdims=True) + w_inter * jnp.sum(q * n_old, axis=-1, keepdims=True)
        den = jnp.maximum(jnp.abs(qn), jnp.exp(-m_t))
        hh = num / den
        m_new = m_t[L - 1:L, :]
        b_last = bcum[L - 1:L, M_HEADS + h:M_HEADS + h + 1]
        w_prev = jnp.exp(b_last + m_prev - m_new)
        kw = k * jnp.exp(b_last - b_col + i_col - m_new)
        cs_ref[h] = w_prev * c_old + lax.dot_general(kw.astype(BF16), v, (((0,), (0,)), ((), ())),
                                                     preferred_element_type=F32)
        ns_ref[h] = w_prev * n_old + jnp.sum(kw, axis=0, keepdims=True)
        ms_ref[h] = jnp.broadcast_to(m_new, (1, 128))
        hn = hh * lax.rsqrt(jnp.mean(hh * hh, axis=-1, keepdims=True) + EPS) * gmh_ref[:, h * M_DV:(h + 1) * M_DV]
        hg_ref[:, h * M_DV:(h + 1) * M_DV] = (hn * _sigmoid(o)).astype(BF16)


    @pl.when(c == nc - 1)
    def _():
        cout_ref[...] = cs_ref[...]
        nout_ref[...] = ns_ref[...]
        mout_ref[...] = ms_ref[...]


def _mlstm(za, zi, seq, w_conv, b_conv, bif, g_mh, c0, n0, m0, cv0):
    b = za.shape[0] // seq
    L = min(seq, 256)
    nc = seq // L
    za3 = za.reshape(b, seq, W_A)
    zi3 = zi.reshape(b, seq, W_GATE)
    shared = c0.shape[0] == 1
    bsel = (lambda bi: 0) if shared else (lambda bi: bi)
    kern = functools.partial(_mlstm_kernel, L=L, nc=nc)
    st4 = lambda last2: pl.BlockSpec((None, M_HEADS) + last2, lambda bi, ci: (bsel(bi), 0, 0, 0))
    out4 = lambda last2: pl.BlockSpec((None, M_HEADS) + last2, lambda bi, ci: (bi, 0, 0, 0))
    return pl.pallas_call(
        kern,
        out_shape=(jax.ShapeDtypeStruct((b, seq, M_V), BF16),
                   jax.ShapeDtypeStruct((b, M_HEADS, M_DK, M_DV), F32),
                   jax.ShapeDtypeStruct((b, M_HEADS, 1, M_DK), F32),
                   jax.ShapeDtypeStruct((b, M_HEADS, 1, 128), F32)),
        grid=(b, nc),
        in_specs=[pl.BlockSpec((None, L, W_A), lambda bi, ci: (bi, ci, 0)),
                  pl.BlockSpec((None, L, W_GATE), lambda bi, ci: (bi, ci, 0)),
                  _resident((CONV_W, 2 * M_QK)), _resident((1, 2 * M_QK)), _resident((1, W_GATE)),
                  _resident((1, M_V)),
                  st4((M_DK, M_DV)), st4((1, M_DK)), st4((1, 128)),
                  pl.BlockSpec((None, 8, 2 * M_QK), lambda bi, ci: (bsel(bi), 0, 0))],
        out_specs=(pl.BlockSpec((None, L, M_V), lambda bi, ci: (bi, ci, 0)),
                   out4((M_DK, M_DV)), out4((1, M_DK)), out4((1, 128))),
        scratch_shapes=[pltpu.VMEM((M_HEADS, M_DK, M_DV), F32), pltpu.VMEM((M_HEADS, 1, M_DK), F32),
                        pltpu.VMEM((M_HEADS, 1, 128), F32), pltpu.VMEM((8, 2 * M_QK), F32)],
        compiler_params=_cparams(("parallel", "arbitrary")),
        name="mlstm",
    )(za3, zi3, w_conv, b_conv, bif, g_mh, c0, n0, m0, cv0)


def _attn_kernel(cq_ref, ct_ref, st_ref, wn_ref, wr_ref, wrs_ref, wuk_ref, wuv_ref,
                 plat_ref, pkr_ref, klat_ref, kkr_ref, o_ref,
                 ql_ref, qr_ref, s0_ref, s_ref, mrun_ref, mb_ref, lrun_ref, acc_ref,
                 *, tq, tk, n_prefix, n_kt, causal):
    i = pl.program_id(1)
    r = A_HEADS * tq
    tdims = (((1,), (1,)), ((), ()))
    lane_chunks = tk // 128

    cq = cq_ref[...]
    qn_all = jnp.dot(cq, wn_ref[...], preferred_element_type=F32)
    qr_all = jnp.dot(cq, wr_ref[...], preferred_element_type=F32)
    qrs_all = jnp.dot(cq, wrs_ref[...], preferred_element_type=F32)
    ct = ct_ref[...]
    st = st_ref[...]
    for h in range(A_HEADS):
        qn = qn_all[:, h * NOPE_DIM:(h + 1) * NOPE_DIM].astype(BF16)
        ql = jnp.dot(qn, wuk_ref[h], preferred_element_type=F32) * QUERY_SCALE
        ql_ref[h * tq:(h + 1) * tq, :] = ql.astype(BF16)
        qr = (qr_all[:, h * 128:(h + 1) * 128] * ct + qrs_all[:, h * 128:(h + 1) * 128] * st) * QUERY_SCALE
        qr_ref[h * tq:(h + 1) * tq, :] = qr.astype(BF16)

    def scores(kl, kk):
        return (lax.dot_general(ql_ref[...], kl, tdims, preferred_element_type=F32)
                + lax.dot_general(qr_ref[:, 0:ROPE_DIM], kk, tdims, preferred_element_type=F32))

    pl_b = plat_ref[...].astype(BF16)
    s0 = scores(pl_b, pkr_ref[...].astype(BF16))
    pcol = lax.broadcasted_iota(jnp.int32, s0.shape, 1)
    s0 = jnp.where(pcol < n_prefix, s0, -jnp.inf)
    s0_ref[...] = s0
    mrun_ref[...] = s0

    def score_block(j, width, masked):
        start = pl.multiple_of(j * tk, tk)
        kl = klat_ref[pl.ds(start, width), :].astype(BF16)
        kk = kkr_ref[pl.ds(start, width), :].astype(BF16)
        s = scores(kl, kk)
        if masked:
            qpos = i * tq + (lax.broadcasted_iota(jnp.int32, s.shape, 0) & (tq - 1))
            kpos = start + lax.broadcasted_iota(jnp.int32, s.shape, 1)
            shift = CHUNK.bit_length() - 1
            s = jnp.where(jnp.right_shift(kpos, shift) <= jnp.right_shift(qpos, shift), s, -jnp.inf)
        s_ref[j, :, 0:width] = s
        mr = mrun_ref[...]
        for c in range(width // 128):
            mr = jnp.maximum(mr, s[:, c * 128:(c + 1) * 128])
        mrun_ref[...] = mr

    def score_body(j, carry):
        score_block(j, tk, False)
        return carry

    if causal:
        last = (i * tq) // tk
        groups = (i * tq - last * tk + tq + 127) // 128
        lax.fori_loop(0, last, score_body, 0)
        for v in range(1, lane_chunks + 1):
            @pl.when(groups == v)
            def _():
                score_block(last, v * 128, True)
        n_full = last
    else:
        lax.fori_loop(0, n_kt, score_body, 0)
        n_full = n_kt

    mb = jnp.broadcast_to(jnp.max(mrun_ref[...], axis=-1, keepdims=True), (r, 128))
    mb_ref[...] = mb
    p0 = jnp.exp2(s0_ref[...] - mb)
    lrun_ref[...] = p0
    acc_ref[...] = jnp.dot(p0.astype(BF16), pl_b, preferred_element_type=F32)

    def value_block(j, width):
        start = pl.multiple_of(j * tk, tk)
        kl = klat_ref[pl.ds(start, width), :].astype(BF16)
        mbw = jnp.concatenate([mb_ref[...]] * (width // 128), axis=1)
        p = jnp.exp2(s_ref[j, :, 0:width] - mbw)
        lr = lrun_ref[...]
        for c in range(width // 128):
            lr = lr + p[:, c * 128:(c + 1) * 128]
        lrun_ref[...] = lr
        acc_ref[...] += jnp.dot(p.astype(BF16), kl, preferred_element_type=F32)

    def value_body(j, carry):
        value_block(j, tk)
        return carry

    lax.fori_loop(0, n_full, value_body, 0)
    if causal:
        for v in range(1, lane_chunks + 1):
            @pl.when(groups == v)
            def _():
                value_block(last, v * 128)

    o = acc_ref[...] / jnp.sum(lrun_ref[...], axis=-1, keepdims=True)
    for h in range(A_HEADS):
        oh = o[h * tq:(h + 1) * tq, :].astype(BF16)
        o_ref[:, h * V_HEAD:(h + 1) * V_HEAD] = jnp.dot(oh, wuv_ref[h], preferred_element_type=F32).astype(BF16)


def _attention(cqn, seq, ctab, stab, wn, wr, wrs, wuk, wuv, plat, pkr, n_prefix, klat, kkr, causal):
    b = cqn.shape[0] // seq
    tq = min(seq, 256)
    nq = seq // tq
    tkeys = klat.shape[1]
    tk = 512
    n_kt = tkeys // tk
    r = A_HEADS * tq
    cq3 = cqn.reshape(b, seq, Q_LORA)
    psel = (lambda bi: 0) if plat.shape[0] == 1 else (lambda bi: bi)
    kern = functools.partial(_attn_kernel, tq=tq, tk=tk, n_prefix=n_prefix, n_kt=n_kt, causal=causal)
    tab = pl.BlockSpec((tq, 128), lambda bi, qi: (qi, 0))
    out = pl.pallas_call(
        kern,
        out_shape=jax.ShapeDtypeStruct((b, seq, A_HEADS * V_HEAD), BF16),
        grid=(b, nq),
        in_specs=[pl.BlockSpec((None, tq, Q_LORA), lambda bi, qi: (bi, qi, 0)), tab, tab,
                  _resident((Q_LORA, A_HEADS * NOPE_DIM)), _resident((Q_LORA, A_HEADS * 128)),
                  _resident((Q_LORA, A_HEADS * 128)), _resident((A_HEADS, NOPE_DIM, KV_LORA)),
                  _resident((A_HEADS, KV_LORA, V_HEAD)),
                  pl.BlockSpec((None, PREFIX_PAD, KV_LORA), lambda bi, qi: (psel(bi), 0, 0)),
                  pl.BlockSpec((None, PREFIX_PAD, ROPE_DIM), lambda bi, qi: (psel(bi), 0, 0)),
                  pl.BlockSpec((None, tkeys, KV_LORA), lambda bi, qi: (bi, 0, 0)),
                  pl.BlockSpec((None, tkeys, ROPE_DIM), lambda bi, qi: (bi, 0, 0))],
        out_specs=pl.BlockSpec((None, tq, A_HEADS * V_HEAD), lambda bi, qi: (bi, qi, 0)),
        scratch_shapes=[pltpu.VMEM((r, KV_LORA), BF16), pltpu.VMEM((r, 128), BF16),
                        pltpu.VMEM((r, PREFIX_PAD), F32), pltpu.VMEM((n_kt, r, tk), F32),
                        pltpu.VMEM((r, 128), F32), pltpu.VMEM((r, 128), F32), pltpu.VMEM((r, 128), F32),
                        pltpu.VMEM((r, KV_LORA), F32)],
        compiler_params=_cparams(("parallel", "arbitrary")),
        name="attention",
    )(cq3, ctab, stab, wn, wr, wrs, wuk, wuv, plat, pkr, klat, kkr)
    return out.reshape(b * seq, A_HEADS * V_HEAD)


ROW_TILE = (D_MODEL // 128, 128)


def _store_row_tiles(ref, rows):
    n = rows.shape[0]
    for s in range(ROW_TILE[0]):
        ref[pl.ds(s, n, stride=ROW_TILE[0]), :] = rows[:, s * 128:(s + 1) * 128]


def _load_row_tiles(ref, n):
    return [ref[pl.ds(s, n, stride=ROW_TILE[0]), :] for s in range(ROW_TILE[0])]


def _merge_kernel(hg_ref, ov_ref, gab_ref, x_ref, wa_ref, wb_ref, wo_ref, gf_ref, wrh_ref, wrl_ref, br_ref,
                  x1_ref, x2_ref, ti_ref, tg_ref, tr_ref, cnt_ref, carry_ref):
    i = pl.program_id(0)

    @pl.when(i == 0)
    def _():
        carry_ref[...] = jnp.zeros_like(carry_ref)

    tm = x_ref.shape[0]
    ya = jnp.dot(hg_ref[...], wa_ref[...], preferred_element_type=F32)
    yb = jnp.dot(ov_ref[...], wb_ref[...], preferred_element_type=F32)
    mixed = (_sigmoid(gab_ref[:, 0:D_MODEL].astype(F32)) * ya
             + _sigmoid(gab_ref[:, D_MODEL:2 * D_MODEL].astype(F32)) * yb)
    x1 = x_ref[...] + jnp.dot(mixed.astype(BF16), wo_ref[...], preferred_element_type=F32)
    x1_ref[...] = x1
    x2 = x1 * lax.rsqrt(jnp.mean(x1 * x1, axis=-1, keepdims=True) + EPS) * gf_ref[...]
    _store_row_tiles(x2_ref, x2)

    lane = lax.broadcasted_iota(jnp.int32, (tm, 128), 1)
    lane_f = lane.astype(F32)
    x2_hi = x2.astype(BF16)
    x2_lo = (x2 - x2_hi.astype(F32)).astype(BF16)
    logits = (jnp.dot(x2_hi, wrh_ref[...], preferred_element_type=F32)
              + jnp.dot(x2_lo, wrh_ref[...], preferred_element_type=F32)
              + jnp.dot(x2_hi, wrl_ref[...], preferred_element_type=F32) + br_ref[...])
    cur = jnp.where(lane < N_EXPERTS, logits, -jnp.inf)
    vals, idxs = [], []
    for _ in range(TOP_K):
        mx = jnp.max(cur, axis=-1, keepdims=True)
        idx = jnp.min(jnp.where(cur == mx, lane_f, 128.0), axis=-1, keepdims=True)
        vals.append(mx)
        idxs.append(idx)
        cur = jnp.where(lane_f == idx, -jnp.inf, cur)
    es = [jnp.exp(v - vals[0]) for v in vals]
    tot = es[0] + es[1] + es[2] + es[3]

    onehots = [(lane_f == idx) for idx in idxs]
    cnt = jnp.zeros((tm, 128), F32)
    for oh in onehots:
        cnt = cnt + jnp.where(oh, 1.0, 0.0)
    rr = lax.broadcasted_iota(jnp.int32, (tm, tm), 0)
    cc = lax.broadcasted_iota(jnp.int32, (tm, tm), 1)
    before = jnp.dot((rr > cc).astype(BF16), cnt.astype(BF16), preferred_element_type=F32) + carry_ref[0:1, :]
    ti = jnp.zeros((tm, 128), F32)
    tg = jnp.zeros((tm, 128), F32)
    tr = jnp.zeros((tm, 128), F32)
    for k in range(TOP_K):
        rank = jnp.sum(jnp.where(onehots[k], before, 0.0), axis=-1, keepdims=True)
        ti = jnp.where(lane == k, idxs[k], ti)
        tg = jnp.where(lane == k, es[k] / tot, tg)
        tr = jnp.where(lane == k, rank, tr)
    ti_ref[...] = ti.astype(jnp.int32)
    tg_ref[...] = tg
    tr_ref[...] = tr.astype(jnp.int32)
    new_carry = carry_ref[0:1, :] + jnp.sum(cnt, axis=0, keepdims=True)
    carry_ref[...] = jnp.broadcast_to(new_carry, carry_ref.shape)
    cnt_ref[...] = jnp.broadcast_to(new_carry, cnt_ref.shape).astype(jnp.int32)


def _merge_route(hg, ov, gab, x2d, wa, wb, wo, g_ffn, w_router_hi, w_router_lo, b_router):
    m = x2d.shape[0]
    tm = min(512, m)
    row = lambda w: pl.BlockSpec((tm, w), lambda i: (i, 0))
    sq = _resident((D_MODEL, D_MODEL))
    return pl.pallas_call(
        _merge_kernel,
        out_shape=(jax.ShapeDtypeStruct((m, D_MODEL), F32), jax.ShapeDtypeStruct((m * ROW_TILE[0], 128), F32),
                   jax.ShapeDtypeStruct((m, 128), jnp.int32), jax.ShapeDtypeStruct((m, 128), F32),
                   jax.ShapeDtypeStruct((m, 128), jnp.int32), jax.ShapeDtypeStruct((8, 128), jnp.int32)),
        grid=(m // tm,),
        in_specs=[row(M_V), row(A_HEADS * V_HEAD), row(W_G), row(D_MODEL), sq, sq, sq,
                  _resident((1, D_MODEL)), _resident((D_MODEL, 128)), _resident((D_MODEL, 128)),
                  _resident((1, 128))],
        out_specs=(row(D_MODEL), pl.BlockSpec((tm * ROW_TILE[0], 128), lambda i: (i, 0)), row(128), row(128),
                   row(128), pl.BlockSpec((8, 128), lambda i: (0, 0))),
        scratch_shapes=[pltpu.VMEM((8, 128), F32)],
        compiler_params=_cparams(("arbitrary",)),
        name="merge_route",
    )(hg, ov, gab, x2d, wa, wb, wo, g_ffn, w_router_hi, w_router_lo, b_router)


def _dispatch_kernel(zt_ref, dest_ref, x_ref, *refs):
    xs_ref, zbuf_ref, sem, zsem = refs[-4:]
    i = pl.program_id(0)
    tr = x_ref.shape[0]
    tm = zbuf_ref.shape[0]

    @pl.when((i == 0) & (zt_ref[0] > 0))
    def _():
        zbuf_ref[...] = jnp.zeros_like(zbuf_ref)

        def start_zero(j, carry):
            start = pl.multiple_of(zt_ref[1 + j], tm)
            pltpu.make_async_copy(zbuf_ref, xs_ref.at[pl.ds(start, tm)], zsem).start()
            return carry

        def wait_zero(j, carry):
            pltpu.make_async_copy(zbuf_ref, xs_ref.at[pl.ds(0, tm)], zsem).wait()
            return carry

        lax.fori_loop(0, zt_ref[0], start_zero, 0)
        lax.fori_loop(0, zt_ref[0], wait_zero, 0)

    def issue(r, carry):
        for k in range(TOP_K):
            pltpu.make_async_copy(x_ref.at[r], xs_ref.at[dest_ref[0, 0, r * TOP_K + k]],
                                  sem).start(priority=k % 2)
        return carry

    lax.fori_loop(0, tr, issue, 0, unroll=8)
    for _ in range(TOP_K):
        pltpu.make_async_copy(x_ref, xs_ref.at[pl.ds(0, tr)], sem).wait()


def _dispatch(x2t, dest, zero_tiles, n_rows, xs_prev=None):
    m = x2t.shape[0] // ROW_TILE[0]
    tr = min(256, m)
    nt = m // tr
    dest3 = dest.reshape(nt, 1, tr * TOP_K)
    in_specs = [pl.BlockSpec((1, 1, tr * TOP_K), lambda i, zt: (i, 0, 0), memory_space=pltpu.SMEM),
                pl.BlockSpec((tr,) + ROW_TILE, lambda i, zt: (i, 0, 0))]
    args = [zero_tiles, dest3, x2t.reshape((m,) + ROW_TILE)]
    aliases = {}
    if xs_prev is not None:
        in_specs.append(pl.BlockSpec(memory_space=pl.ANY))
        args.append(xs_prev)
        aliases = {3: 0}
    grid_spec = pltpu.PrefetchScalarGridSpec(
        num_scalar_prefetch=1,
        grid=(nt,),
        in_specs=in_specs,
        out_specs=pl.BlockSpec(memory_space=pl.ANY),
        scratch_shapes=[pltpu.VMEM((MOE_TILE,) + ROW_TILE, F32), pltpu.SemaphoreType.DMA(()),
                        pltpu.SemaphoreType.DMA(())],
    )
    return pl.pallas_call(
        _dispatch_kernel,
        out_shape=jax.ShapeDtypeStruct((n_rows,) + ROW_TILE, F32),
        grid_spec=grid_spec,
        input_output_aliases=aliases,
        compiler_params=_cparams(("arbitrary",)),
        name="moe_dispatch",
    )(*args)


def _expert_kernel(te_ref, nu_ref, xs_ref, wgu_ref, bgu_ref, wdn_ref, bdn_ref, ys_ref, wgu_b, wdn_b):
    i = pl.program_id(0)
    prev = te_ref[jnp.maximum(i - 1, 0)]

    @pl.when(i < nu_ref[0])
    def _():
        @pl.when((i == 0) | (te_ref[i] != prev))
        def _():
            rc = 128
            for c in range(D_MODEL // rc):
                wgu_b[c * rc:(c + 1) * rc, :] = wgu_ref[c * rc:(c + 1) * rc, :].astype(BF16)
                wdn_b[c * rc:(c + 1) * rc, :] = wdn_ref[c * rc:(c + 1) * rc, :].astype(BF16)

        tm = xs_ref.shape[0] // ROW_TILE[0]
        xb = jnp.concatenate([c.astype(BF16) for c in _load_row_tiles(xs_ref, tm)], axis=1)
        fc = 512
        y = jnp.zeros((tm, D_MODEL), F32) + bdn_ref[...]
        for f0 in range(0, D_FF, fc):
            def proj(c0):
                return jnp.dot(xb, wgu_b[:, c0:c0 + fc], preferred_element_type=F32) + bgu_ref[:, c0:c0 + fc]
            gate = jnp.minimum(proj(f0), SWIGLU_LIMIT)
            up = jnp.clip(proj(D_FF + f0), -SWIGLU_LIMIT, SWIGLU_LIMIT)
            hid = (up + 1.0) * gate * _sigmoid(SWIGLU_ALPHA * gate)
            y = y + jnp.dot(hid.astype(BF16), wdn_b[f0:f0 + fc, :], preferred_element_type=F32)
        _store_row_tiles(ys_ref, y)

    @pl.when(i >= nu_ref[0])
    def _():
        ys_ref[...] = jnp.zeros_like(ys_ref)


def _experts(xs, tile_expert, n_used, w_gu, b_gu, w_dn, b_dn):
    n_rows, w = xs.shape
    tm = MOE_TILE * ROW_TILE[0]
    nt = n_rows // tm
    tile = lambda i, te, nu: (jnp.minimum(i, jnp.maximum(nu[0] - 1, 0)), 0)
    grid_spec = pltpu.PrefetchScalarGridSpec(
        num_scalar_prefetch=2,
        grid=(nt,),
        in_specs=[pl.BlockSpec((tm, w), tile),
                  pl.BlockSpec((None, D_MODEL, 2 * D_FF), lambda i, te, nu: (te[i], 0, 0)),
                  pl.BlockSpec((None, 1, 2 * D_FF), lambda i, te, nu: (te[i], 0, 0)),
                  pl.BlockSpec((None, D_FF, D_MODEL), lambda i, te, nu: (te[i], 0, 0)),
                  pl.BlockSpec((None, 1, D_MODEL), lambda i, te, nu: (te[i], 0, 0))],
        out_specs=pl.BlockSpec((tm, w), lambda i, te, nu: (i, 0)),
        scratch_shapes=[pltpu.VMEM((D_MODEL, 2 * D_FF), BF16), pltpu.VMEM((D_FF, D_MODEL), BF16)],
    )
    return pl.pallas_call(
        _expert_kernel,
        out_shape=jax.ShapeDtypeStruct((n_rows, w), F32),
        grid_spec=grid_spec,
        compiler_params=_cparams(("arbitrary",)),
        name="moe_experts",
    )(tile_expert, n_used, xs, w_gu, b_gu, w_dn, b_dn)


def _combine_kernel(dcur_ref, dnext_ref, x1_ref, tg_ref, gfin_ref, ys_ref, y_ref, buf_ref, sem_ref, *, nt):
    i = pl.program_id(0)
    tc = x1_ref.shape[0]

    def issue(dref, slot):
        def body(r, carry):
            for k in range(TOP_K):
                src0 = pl.multiple_of(dref[0, 0, r * TOP_K + k] * ROW_TILE[0], ROW_TILE[0])
                dst0 = pl.multiple_of(r * ROW_TILE[0], ROW_TILE[0])
                pltpu.make_async_copy(ys_ref.at[pl.ds(src0, ROW_TILE[0]), :],
                                      buf_ref.at[slot, k, pl.ds(dst0, ROW_TILE[0]), :],
                                      sem_ref.at[slot]).start(priority=k % 2)
            return carry
        lax.fori_loop(0, tc, body, 0, unroll=8)

    slot = i % 2

    @pl.when(i == 0)
    def _():
        issue(dcur_ref, 0)

    @pl.when(i + 1 < nt)
    def _():
        issue(dnext_ref, 1 - slot)

    for k in range(TOP_K):
        pltpu.make_async_copy(ys_ref.at[pl.ds(0, tc * ROW_TILE[0]), :], buf_ref.at[slot, k],
                              sem_ref.at[slot]).wait()

    gates = [tg_ref[:, k:k + 1] for k in range(TOP_K)]
    chunks = []
    ssq = jnp.zeros((tc, 1), F32)
    for s in range(ROW_TILE[0]):
        xc = x1_ref[:, s * 128:(s + 1) * 128]
        for k in range(TOP_K):
            xc = xc + gates[k] * buf_ref[slot, k, pl.ds(s, tc, stride=ROW_TILE[0]), :]
        chunks.append(xc)
        ssq = ssq + jnp.sum(xc * xc, axis=-1, keepdims=True)
    inv = lax.rsqrt(ssq / D_MODEL + EPS)
    for s in range(ROW_TILE[0]):
        y_ref[:, s * 128:(s + 1) * 128] = chunks[s] * inv * gfin_ref[:, s * 128:(s + 1) * 128]


def _combine(dest, x1, tg, g_final, ys):
    m = x1.shape[0]
    tc = min(256, m)
    nt = m // tc
    dest3 = dest.reshape(nt, 1, tc * TOP_K)
    kern = functools.partial(_combine_kernel, nt=nt)
    smem = lambda f: pl.BlockSpec((1, 1, tc * TOP_K), f, memory_space=pltpu.SMEM)
    return pl.pallas_call(
        kern,
        out_shape=jax.ShapeDtypeStruct((m, D_MODEL), F32),
        grid=(nt,),
        in_specs=[smem(lambda i: (i, 0, 0)), smem(lambda i: (jnp.minimum(i + 1, nt - 1), 0, 0)),
                  pl.BlockSpec((tc, D_MODEL), lambda i: (i, 0)), pl.BlockSpec((tc, 128), lambda i: (i, 0)),
                  _resident((1, D_MODEL)), pl.BlockSpec(memory_space=pl.ANY)],
        out_specs=pl.BlockSpec((tc, D_MODEL), lambda i: (i, 0)),
        scratch_shapes=[pltpu.VMEM((2, TOP_K, tc * ROW_TILE[0], 128), F32), pltpu.SemaphoreType.DMA((2,))],
        compiler_params=_cparams(("arbitrary",)),
        name="moe_combine",
    )(dest3, dest3, x1, tg, g_final, ys)


def _rope_tables(pos):
    half = ROPE_DIM // 2
    freqs = ROPE_THETA ** (-jnp.arange(half, dtype=F32) / half)
    ang = pos.astype(F32)[:, None] * freqs[None, :]
    cos, sin = jnp.cos(ang), jnp.sin(ang)
    return jnp.concatenate([cos, cos], axis=-1), jnp.concatenate([-sin, sin], axis=-1)


def _tile_rows(t, rows):
    return t if t.shape[0] >= rows else jnp.tile(t, (rows // t.shape[0], 1))


def _pad_lanes(t, width):
    return jnp.pad(t, ((0, 0), (0, width - t.shape[1])))


def _moe(streams, g_final, w_gu, b_gu, w_dn, b_dn):
    tm = MOE_TILE
    a = sum(s[0].shape[0] for s in streams) * TOP_K
    n_tiles = -(-a // tm) + N_EXPERTS
    counts = [s[5][0, :N_EXPERTS] for s in streams]
    total = sum(counts)
    pcounts = (total + tm - 1) // tm * tm
    pend = jnp.cumsum(pcounts)
    n_used = (pend[-1] // tm).astype(jnp.int32)
    tiles = jnp.minimum(jnp.arange(n_tiles, dtype=jnp.int32), n_used - 1) * tm
    tile_expert = jnp.minimum(jnp.sum(pend[None, :] <= tiles[:, None], axis=-1), N_EXPERTS - 1).astype(jnp.int32)
    eids = jnp.arange(N_EXPERTS, dtype=jnp.int32)
    base = pend - pcounts
    rest_tiles = (a - streams[0][0].shape[0] * TOP_K) // tm
    first = (base + counts[0]) // tm
    cand = first[:, None] + jnp.arange(rest_tiles + 2, dtype=jnp.int32)[None, :]
    cand = jnp.where((cand < (pend // tm)[:, None]) & (pcounts[:, None] > 0), cand, -1)
    tail = n_used + jnp.arange(N_EXPERTS, dtype=jnp.int32)
    cand = jnp.concatenate([cand.reshape(-1), jnp.where(tail < n_tiles, tail, -1)])
    n_zero = jnp.sum(cand >= 0).astype(jnp.int32)
    zlist = (-jnp.sort(-cand))[:rest_tiles + 3 * N_EXPERTS] * tm
    zero_tiles = jnp.concatenate([n_zero[None], zlist]).astype(jnp.int32)
    dests, xs = [], None
    for s, (_, x2t, ti, _, tr, _) in enumerate(streams):
        dest = tr[:, :TOP_K] + jnp.sum(jnp.where(ti[:, :TOP_K, None] == eids, base, 0), axis=-1)
        dests.append(dest.astype(jnp.int32))
        zt = zero_tiles if s == 0 else jnp.zeros_like(zero_tiles)
        xs = _dispatch(x2t, dests[-1], zt, n_tiles * tm, xs)
        base = base + counts[s]
    ys = _experts(xs.reshape(n_tiles * tm * ROW_TILE[0], 128), tile_expert, n_used.reshape(1),
                  w_gu, b_gu, w_dn, b_dn)
    return [_combine(dest, s[0], s[3], g_final, ys) for dest, s in zip(dests, streams)]


def kernel(x_prompt, x_sample, cache_kv_latent, cache_k_rope, state_mlstm_C, state_mlstm_n, state_mlstm_m, state_conv, meta_tokens, g_mix_norm, w_in, b_if, w_conv, b_conv, g_mh_norm, w_proj_a, g_q_norm, g_kv_norm, w_uq, w_uk, w_uv, w_proj_b, w_out, g_ffn_norm, w_router, b_router, w_gate_up, b_gate_up, w_down, b_down, g_final_norm):
    bsz, seq = x_prompt.shape[0], x_prompt.shape[1]
    dbs, dseq = x_sample.shape[0], x_sample.shape[1]
    past = cache_kv_latent.shape[2]
    assert w_in.shape[0] == 1, "single-layer trunk"
    l = 0

    wi = w_in[l]
    o_gate = 2 * M_QK + 2 * M_V
    o_cq = o_gate + 2 * M_HEADS
    o_ckv = o_cq + Q_LORA
    o_kr = o_ckv + KV_LORA
    o_g = o_kr + ROPE_DIM
    swap = np.concatenate([np.arange(ROPE_DIM // 2, ROPE_DIM), np.arange(ROPE_DIM // 2)])
    w_kr = wi[:, o_kr:o_g]
    w_cat = jnp.concatenate([wi[:, :o_gate], _pad_lanes(wi[:, o_gate:o_cq], W_GATE), wi[:, o_cq:o_kr], w_kr,
                             w_kr[:, swap], wi[:, o_g:]], axis=1).astype(BF16)
    bif = _pad_lanes(b_if[l][None, :], W_GATE)
    uq = w_uq[l].reshape(Q_LORA, A_HEADS, NOPE_DIM + ROPE_DIM)
    wn = uq[:, :, :NOPE_DIM].reshape(Q_LORA, A_HEADS * NOPE_DIM).astype(BF16)
    uq_r = uq[:, :, NOPE_DIM:]
    pad_r = lambda t: jnp.pad(t, ((0, 0), (0, 0), (0, 128 - ROPE_DIM))).reshape(Q_LORA, A_HEADS * 128).astype(BF16)
    wr, wrs = pad_r(uq_r), pad_r(uq_r[:, :, swap])
    wuk = jnp.transpose(w_uk[l], (1, 2, 0)).astype(BF16)
    wuv = jnp.transpose(w_uv[l], (1, 0, 2)).astype(BF16)
    wa, wb, wo = w_proj_a[l].astype(BF16), w_proj_b[l].astype(BF16), w_out[l].astype(BF16)
    w_rt = _pad_lanes(w_router[l], 128)
    w_rt_hi = w_rt.astype(BF16)
    w_rt_lo = (w_rt - w_rt_hi.astype(F32)).astype(BF16)
    b_rt = _pad_lanes(b_router[l][None, :], 128)
    g_mix, g_q, g_kv = g_mix_norm[l][None, :], g_q_norm[l][None, :], g_kv_norm[l][None, :]
    g_mh, g_ffn, g_fin = g_mh_norm[l][None, :], g_ffn_norm[l][None, :], g_final_norm[None, :]
    wcv, bcv = w_conv[l], b_conv[l][None, :]
    w_gu, b_gu = w_gate_up[l], b_gate_up[l][:, None, :]
    w_dn, b_dn = w_down[l], b_down[l][:, None, :]

    ct_m, st_m = _rope_tables(jnp.arange(N_META))
    ct_p, st_p = _rope_tables(N_META + jnp.arange(seq))
    ct_s, st_s = _rope_tables(N_META + past + jnp.arange(dseq))

    def stream_tables(ct, st, m_rows):
        rows = max(ct.shape[0], min(512, m_rows))
        return _tile_rows(ct, rows), _tile_rows(st, rows)

    def prefix(lat, kr):
        pad = ((0, 0), (0, PREFIX_PAD - lat.shape[1]), (0, 0))
        return jnp.pad(lat, pad), jnp.pad(kr, pad)

    za_m, zi_m, _, lat_m, kr_m, _, tail_m = _inproj(meta_tokens, N_META, g_mix, w_cat, g_q, g_kv, ct_m, st_m)
    npad = CHUNK - N_META
    za_mp = jnp.concatenate([jnp.zeros((npad, W_A), BF16), za_m], axis=0)
    lane = jnp.arange(W_GATE)
    neutral = jnp.where(lane < M_HEADS, -1e30, jnp.where(lane < 2 * M_HEADS, 1e30, 0.0)).astype(F32)
    zi_mp = jnp.concatenate([jnp.broadcast_to(neutral, (npad, W_GATE)), zi_m], axis=0)
    zeros_state = (jnp.zeros((1, M_HEADS, M_DK, M_DV), F32), jnp.zeros((1, M_HEADS, 1, M_DK), F32),
                   jnp.zeros((1, M_HEADS, 1, 128), F32), jnp.zeros((1, 8, 2 * M_QK), F32))
    _, c_m, n_m, m_m = _mlstm(za_mp, zi_mp, CHUNK, wcv, bcv, bif, g_mh, *zeros_state)

    def route(x2d, hg, ov, gab):
        return _merge_route(hg.reshape(x2d.shape[0], M_V), ov, gab, x2d, wa, wb, wo, g_ffn, w_rt_hi, w_rt_lo, b_rt)

    xp = x_prompt.reshape(bsz * seq, D_MODEL)
    ctp, stp = stream_tables(ct_p, st_p, bsz * seq)
    za, zi, cqn, lat, kr, gab, tail = _inproj(xp, seq, g_mix, w_cat, g_q, g_kv, ctp, stp)
    hg, c_p, n_p, m_p = _mlstm(za, zi, seq, wcv, bcv, bif, g_mh, c_m, n_m, m_m, tail_m)
    plat, pkr = prefix(lat_m[None], kr_m[None])
    ctq, stq = _pad_lanes(ct_p, 128), _pad_lanes(st_p, 128)
    ov = _attention(cqn, seq, ctq, stq, wn, wr, wrs, wuk, wuv, plat, pkr, N_META,
                    lat.reshape(bsz, seq, KV_LORA), kr.reshape(bsz, seq, ROPE_DIM), True)
    routed_p = route(xp, hg, ov, gab)

    xs2 = x_sample.reshape(dbs * dseq, D_MODEL)
    cts, sts = stream_tables(ct_s, st_s, dbs * dseq)
    za, zi, cqn, lat_s, kr_s, gab, tail_s = _inproj(xs2, dseq, g_mix, w_cat, g_q, g_kv, cts, sts)
    cv0 = jnp.pad(state_conv[l], ((0, 0), (8 - (CONV_W - 1), 0), (0, 0)))
    m0 = jnp.broadcast_to(state_mlstm_m[l][:, :, None, None], (dbs, M_HEADS, 1, 128))
    hg, c_s, n_s, m_s = _mlstm(za, zi, dseq, wcv, bcv, bif, g_mh, state_mlstm_C[l],
                               state_mlstm_n[l][:, :, None, :], m0, cv0)
    plat, pkr = prefix(lat_s.reshape(dbs, dseq, KV_LORA), kr_s.reshape(dbs, dseq, ROPE_DIM))
    ov = _attention(cqn, dseq, _pad_lanes(ct_s, 128), _pad_lanes(st_s, 128), wn, wr, wrs, wuk, wuv, plat, pkr,
                    dseq, cache_kv_latent[l], cache_k_rope[l], False)
    routed_s = route(xs2, hg, ov, gab)

    y_p, y_s = _moe([routed_p, routed_s], g_fin, w_gu, b_gu, w_dn, b_dn)
    y_prompt = y_p.reshape(bsz, seq, D_MODEL)
    y_sample = y_s.reshape(dbs, dseq, D_MODEL)

    def with_meta(meta_rows, frames, width):
        return jnp.concatenate([jnp.broadcast_to(meta_rows[None], (bsz, N_META, width)),
                                frames.reshape(bsz, seq, width)], axis=1)[None]

    p_lat = with_meta(lat_m, lat, KV_LORA)
    p_kr = with_meta(kr_m, kr, ROPE_DIM)
    tail3 = lambda t: t[:, 8 - (CONV_W - 1):, :][None]
    return (y_prompt, y_sample, p_lat, p_kr, c_p[None], n_p[:, :, 0, :][None], m_p[:, :, 0, 0][None], tail3(tail),
            lat_s.reshape(dbs, dseq, KV_LORA)[None], kr_s.reshape(dbs, dseq, ROPE_DIM)[None], c_s[None],
            n_s[:, :, 0, :][None], m_s[:, :, 0, 0][None], tail3(tail_s))
```
